```python
import jax, jax.numpy as jnp
from jax import lax
import numpy as np

D_MODEL = 1024
BATCH = 32
SEQ = 2048
DEPTH = 1
DEC_BATCH = 32
DEC_SEQ = 16
PAST_LEN = 1024

CHUNK = 64
HEAD_DIM = 64
N_HEADS_SB = 8
N_HEADS_FOX = 8
D_SB = N_HEADS_SB * HEAD_DIM
D_FOX = N_HEADS_FOX * HEAD_DIM
D_MIX = D_SB + D_FOX
D_IN = 3 * D_SB + 3 * D_FOX + N_HEADS_FOX
Q_BLOCK = 128
ATTN_SCALE = HEAD_DIM ** -0.5
N_EXPERTS = 32
TOP_K = 4
D_EXPERT = D_MODEL
SWIGLU_LIMIT = 7.0
SWIGLU_ALPHA = 1.702
MOE_BLOCK = 256
D_PLE = 256
FORGET_BIAS_INIT = 2.0
EPS = 1e-6

kernel_name = 'stickbreak_fox_hybrid_moe_stream_step'


def rms_norm(x, g):
    xf = x.astype(jnp.float32)
    y = xf * lax.rsqrt(jnp.mean(xf * xf, axis=-1, keepdims=True) + EPS)
    return (y * g.astype(jnp.float32)).astype(x.dtype)


def head_norm(o, g):
    of = o.astype(jnp.float32)
    y = of * lax.rsqrt(jnp.mean(of * of, axis=-1, keepdims=True) + EPS)
    b, t = o.shape[:2]
    return (y.reshape(b, t, -1) * g.astype(jnp.float32)).astype(o.dtype)


def to_blocks(a):
    b, t = a.shape[:2]
    return jnp.moveaxis(a.reshape(b, t // Q_BLOCK, Q_BLOCK, *a.shape[2:]), 1, 0)


def from_blocks(a):
    a = jnp.moveaxis(a, 0, 1)
    return a.reshape(a.shape[0], -1, *a.shape[3:])


def project(a, w_in, b_f):
    z = a @ w_in
    b, t = a.shape[:2]
    qkv_sb, qkv_fox, f = jnp.split(z, [3 * D_SB, 3 * D_SB + 3 * D_FOX], axis=-1)
    q1, k1, v1 = jnp.moveaxis(qkv_sb.reshape(b, t, 3, N_HEADS_SB, HEAD_DIM), 2, 0)
    q2, k2, v2 = jnp.moveaxis(qkv_fox.reshape(b, t, 3, N_HEADS_FOX, HEAD_DIM), 2, 0)
    logf = jax.nn.log_sigmoid((f + b_f).astype(jnp.float32))
    return q1, k1, v1, q2, k2, v2, logf


def stick_breaking_block(q, k, v, q_pos, k_pos):
    z = jnp.einsum('bqhd,bkhd->bhqk', q, k).astype(jnp.float32) * ATTN_SCALE
    before = k_pos[None, :] < q_pos[:, None]
    log_1m = jnp.where(before, jax.nn.log_sigmoid(-z), 0.0)
    suffix = lax.cumsum(log_1m, axis=3, reverse=True) - log_1m
    w = jnp.where(before, jnp.exp(jax.nn.log_sigmoid(z) + suffix), 0.0)
    return jnp.einsum('bhqk,bkhd->bqhd', w.astype(v.dtype), v)


def forgetting_block(q, k, v, cq, ck, q_pos, k_pos):
    s = jnp.einsum('bqhd,bkhd->bhqk', q, k).astype(jnp.float32) * ATTN_SCALE
    s = s + jnp.swapaxes(cq, 1, 2)[:, :, :, None] - jnp.swapaxes(ck, 1, 2)[:, :, None, :]
    s = jnp.where(k_pos[None, :] <= q_pos[:, None], s, -jnp.inf)
    w = jax.nn.softmax(s, axis=-1)
    return jnp.einsum('bhqk,bkhd->bqhd', w.astype(v.dtype), v)


def stick_breaking_prompt(q, k, v):
    pos = jnp.arange(q.shape[1], dtype=jnp.int32)

    def one(args):
        qb, pb = args
        return stick_breaking_block(qb, k, v, pb, pos)

    return from_blocks(lax.map(one, (to_blocks(q), pos.reshape(-1, Q_BLOCK))))


def forgetting_prompt(q, k, v, c):
    pos = jnp.arange(q.shape[1], dtype=jnp.int32)

    def one(args):
        qb, cb, pb = args
        return forgetting_block(qb, k, v, cb, c, pb, pos)

    return from_blocks(lax.map(one, (to_blocks(q), to_blocks(c), pos.reshape(-1, Q_BLOCK))))


def merge_heads(o1, o2, g_sb, g_fox, w_out):
    return jnp.concatenate([head_norm(o1, g_sb), head_norm(o2, g_fox)], axis=-1) @ w_out


def moe(x, w_router, b_router, w_up, b_up, w_down, b_down):
    b, t, d = x.shape
    xf = x.reshape(-1, d)
    n = xf.shape[0]
    logits = (xf @ w_router + b_router).astype(jnp.float32)
    top_val, top_idx = lax.top_k(logits, TOP_K)
    gates = jax.nn.softmax(top_val, axis=-1)
    flat_e = top_idx.reshape(-1).astype(jnp.int32)
    flat_tok = jnp.arange(n * TOP_K, dtype=jnp.int32) // TOP_K
    order = jnp.argsort(flat_e)
    sorted_e = flat_e[order]
    counts = jnp.zeros((N_EXPERTS,), jnp.int32).at[flat_e].add(1)
    group_start = jnp.cumsum(counts) - counts
    padded = (counts + MOE_BLOCK - 1) // MOE_BLOCK * MOE_BLOCK
    padded_end = jnp.cumsum(padded)
    padded_start = padded_end - padded
    rank = jnp.arange(n * TOP_K, dtype=jnp.int32) - group_start[sorted_e]
    dest_sorted = padded_start[sorted_e] + rank
    dest = jnp.zeros_like(dest_sorted).at[order].set(dest_sorted)
    n_blocks = -(-(n * TOP_K) // MOE_BLOCK) + N_EXPERTS
    cap = n_blocks * MOE_BLOCK
    slot_tok = jnp.full((cap,), n, jnp.int32).at[dest].set(flat_tok)
    block_start = jnp.arange(n_blocks, dtype=jnp.int32) * MOE_BLOCK
    block_e = jnp.minimum(jnp.searchsorted(padded_end, block_start, side='right'), N_EXPERTS - 1)
    x_pad = jnp.concatenate([xf, jnp.zeros((1, d), xf.dtype)], axis=0)

    def run_block(args):
        tok, e = args
        xb = x_pad[tok]
        gu = xb @ w_up[e] + b_up[e]
        gate = jnp.minimum(gu[:, :D_EXPERT], SWIGLU_LIMIT)
        up = jnp.clip(gu[:, D_EXPERT:], -SWIGLU_LIMIT, SWIGLU_LIMIT)
        hb = (up + 1) * gate * jax.nn.sigmoid(SWIGLU_ALPHA * gate)
        return hb @ w_down[e] + b_down[e]

    y_slots = lax.map(run_block, (slot_tok.reshape(n_blocks, MOE_BLOCK), block_e)).reshape(cap, d)
    y = jnp.einsum('nk,nkd->nd', gates.astype(x.dtype), y_slots[dest].reshape(n, TOP_K, d))
    return y.reshape(b, t, d)


def ffn_and_ple(h, p, g_ffn, w_router, b_router, w_up, b_up, w_down, b_down, w_ple, w_ple_gate, b_ple_gate):
    h = h + moe(rms_norm(h, g_ffn), w_router, b_router, w_up, b_up, w_down, b_down)
    gate = jax.nn.sigmoid(h @ w_ple_gate + b_ple_gate)
    return h + gate * (p @ w_ple)


def setup_inputs(seed: int = 0) -> dict:
    key = jax.random.key(seed)
    ks = jax.random.split(key, 32)

    def nrm(k, shape, scale):
        return jax.random.normal(k, shape, jnp.float32) * scale

    L = DEPTH
    kv_c = (L, DEC_BATCH, PAST_LEN, N_HEADS_SB, HEAD_DIM)
    kv_f = (L, DEC_BATCH, PAST_LEN, N_HEADS_FOX, HEAD_DIM)
    return {
        'x_prompt': nrm(ks[0], (BATCH, SEQ, D_MODEL), 1.0),
        'x_sample': nrm(ks[1], (DEC_BATCH, DEC_SEQ, D_MODEL), 1.0),
        'cache_sb_k': nrm(ks[2], kv_c, 1.0),
        'cache_sb_v': nrm(ks[3], kv_c, 1.0),
        'cache_fox_k': nrm(ks[4], kv_f, 1.0),
        'cache_fox_v': nrm(ks[5], kv_f, 1.0),
        'cache_fox_logf': jax.nn.log_sigmoid(FORGET_BIAS_INIT + nrm(ks[6], (L, DEC_BATCH, PAST_LEN, N_HEADS_FOX), 1.0)),
        'p_prompt': nrm(ks[7], (L, BATCH, SEQ, D_PLE), 1.0),
        'p_sample': nrm(ks[8], (L, DEC_BATCH, DEC_SEQ, D_PLE), 1.0),
        'g_attn': 1.0 + nrm(ks[9], (L, D_MODEL), 0.02),
        'w_in': nrm(ks[10], (L, D_MODEL, D_IN), D_MODEL ** -0.5),
        'b_f': FORGET_BIAS_INIT + nrm(ks[11], (L, N_HEADS_FOX), 0.5),
        'g_sb': 1.0 + nrm(ks[12], (L, D_SB), 0.02),
        'g_fox': 1.0 + nrm(ks[13], (L, D_FOX), 0.02),
        'w_out': nrm(ks[14], (L, D_MIX, D_MODEL), D_MIX ** -0.5),
        'g_ffn': 1.0 + nrm(ks[15], (L, D_MODEL), 0.02),
        'w_router': nrm(ks[16], (L, D_MODEL, N_EXPERTS), D_MODEL ** -0.5),
        'b_router': nrm(ks[17], (L, N_EXPERTS), 0.01),
        'w_up': nrm(ks[18], (L, N_EXPERTS, D_MODEL, 2 * D_EXPERT), D_MODEL ** -0.5),
        'b_up': nrm(ks[19], (L, N_EXPERTS, 2 * D_EXPERT), 0.01),
        'w_down': nrm(ks[20], (L, N_EXPERTS, D_EXPERT, D_MODEL), D_EXPERT ** -0.5),
        'b_down': nrm(ks[21], (L, N_EXPERTS, D_MODEL), 0.01),
        'w_ple': nrm(ks[22], (L, D_PLE, D_MODEL), D_PLE ** -0.5),
        'w_ple_gate': nrm(ks[23], (L, D_MODEL, D_MODEL), D_MODEL ** -0.5),
        'b_ple_gate': nrm(ks[24], (L, D_MODEL), 0.01),
        'g_final': 1.0 + nrm(ks[25], (D_MODEL,), 0.02),
    }


def reference(x_prompt, x_sample, cache_sb_k, cache_sb_v, cache_fox_k, cache_fox_v, cache_fox_logf,
              p_prompt, p_sample, g_attn, w_in, b_f, g_sb, g_fox, w_out, g_ffn, w_router, b_router,
              w_up, b_up, w_down, b_down, w_ple, w_ple_gate, b_ple_gate, g_final):
    hp, hs = x_prompt, x_sample
    sbk_p, sbv_p, fk_p, fv_p, fl_p = [], [], [], [], []
    sbk_s, sbv_s, fk_s, fv_s, fl_s = [], [], [], [], []
    past = cache_sb_k.shape[2]
    t_s = x_sample.shape[1]
    k_pos_s = jnp.arange(past + t_s, dtype=jnp.int32)
    q_pos_s = past + jnp.arange(t_s, dtype=jnp.int32)
    for i in range(DEPTH):
        a = rms_norm(hp, g_attn[i])
        q1, k1, v1, q2, k2, v2, lf = project(a, w_in[i], b_f[i])
        o1 = stick_breaking_prompt(q1, k1, v1)
        o2 = forgetting_prompt(q2, k2, v2, jnp.cumsum(lf, axis=1))
        hp = hp + merge_heads(o1, o2, g_sb[i], g_fox[i], w_out[i])
        hp = ffn_and_ple(hp, p_prompt[i], g_ffn[i], w_router[i], b_router[i], w_up[i], b_up[i],
                         w_down[i], b_down[i], w_ple[i], w_ple_gate[i], b_ple_gate[i])
        sbk_p.append(k1); sbv_p.append(v1); fk_p.append(k2); fv_p.append(v2); fl_p.append(lf)

        a = rms_norm(hs, g_attn[i])
        q1, k1, v1, q2, k2, v2, lf = project(a, w_in[i], b_f[i])
        k1a = jnp.concatenate([cache_sb_k[i], k1], axis=1)
        v1a = jnp.concatenate([cache_sb_v[i], v1], axis=1)
        o1 = stick_breaking_block(q1, k1a, v1a, q_pos_s, k_pos_s)
        k2a = jnp.concatenate([cache_fox_k[i], k2], axis=1)
        v2a = jnp.concatenate([cache_fox_v[i], v2], axis=1)
        ca = jnp.cumsum(jnp.concatenate([cache_fox_logf[i].astype(jnp.float32), lf], axis=1), axis=1)
        o2 = forgetting_block(q2, k2a, v2a, ca[:, past:], ca, q_pos_s, k_pos_s)
        hs = hs + merge_heads(o1, o2, g_sb[i], g_fox[i], w_out[i])
        hs = ffn_and_ple(hs, p_sample[i], g_ffn[i], w_router[i], b_router[i], w_up[i], b_up[i],
                         w_down[i], b_down[i], w_ple[i], w_ple_gate[i], b_ple_gate[i])
        sbk_s.append(k1); sbv_s.append(v1); fk_s.append(k2); fv_s.append(v2); fl_s.append(lf)

    y_prompt = rms_norm(hp, g_final)
    y_sample = rms_norm(hs, g_final)
    return (y_prompt, y_sample,
            jnp.stack(sbk_p), jnp.stack(sbv_p), jnp.stack(fk_p), jnp.stack(fv_p), jnp.stack(fl_p),
            jnp.stack(sbk_s), jnp.stack(sbv_s), jnp.stack(fk_s), jnp.stack(fv_s), jnp.stack(fl_s))
```

```python
import functools

import jax
import jax.numpy as jnp
from jax import lax
from jax.experimental import pallas as pl
from jax.experimental.pallas import tpu as pltpu

F32 = jnp.float32
BF16 = jnp.bfloat16
I32 = jnp.int32

D_MODEL = 1024
N_HEADS = 8
HEAD_DIM = 64
D_GRP = N_HEADS * HEAD_DIM
N_PAIRS = N_HEADS // 2
ATTN_SCALE = HEAD_DIM ** -0.5
N_EXPERTS = 32
TOP_K = 4
D_EXPERT = D_MODEL
SWIGLU_LIMIT = 7.0
SWIGLU_ALPHA = 1.702
D_PLE = 256
EPS = 1e-6

LANES = 128
SUBLANES = 8
ROW_TILES = D_MODEL // LANES
VMEM_LIMIT_BYTES = 56 * 1024 * 1024
NEG_BIG = -1e30

TOKEN_TILE = 512
ATTN_TILE = 256
NEW_KEY_PAD = 128
MOE_ROWS = 512


def _dot(a, b):
    return jnp.dot(a, b, preferred_element_type=F32)


def _dot_nt(a, b):
    return lax.dot_general(a, b, (((1,), (1,)), ((), ())), preferred_element_type=F32)


def _split2(x):
    hi = x.astype(BF16)
    lo = (x - hi.astype(F32)).astype(BF16)
    return hi, lo


def _split3(x):
    hi = x.astype(BF16)
    r = x - hi.astype(F32)
    mid = r.astype(BF16)
    lo = (r - mid.astype(F32)).astype(BF16)
    return hi, mid, lo


def _dot3_left(m_bf16, x):
    hi, mid, lo = _split3(x)
    return _dot(m_bf16, hi) + _dot(m_bf16, mid) + _dot(m_bf16, lo)


def _dot3_right(x, m_bf16):
    hi, mid, lo = _split3(x)
    return _dot(hi, m_bf16) + _dot(mid, m_bf16) + _dot(lo, m_bf16)


def _rms(x, g):
    return x * lax.rsqrt(jnp.mean(x * x, axis=-1, keepdims=True) + EPS) * g


def _softplus(z):
    return jnp.maximum(z, 0.0) + jnp.log(1.0 + jnp.exp(-jnp.abs(z)))


def _params(*sem):
    return pltpu.CompilerParams(dimension_semantics=sem, vmem_limit_bytes=VMEM_LIMIT_BYTES)


def _inproj_kernel(x_ref, g_ref, wqkv_ref, wf_ref, bf_ref, tri_ref,
                   q1_ref, k1_ref, v1_ref, q2_ref, k2_ref, v2_ref, lf_ref,
                   kb1_ref, vb1_ref, kb2_ref, vb2_ref, c_ref, ct_ref, lft_ref,
                   carry_ref, *, tiles_per_seq):
    i = pl.program_id(0)
    a = _rms(x_ref[...], g_ref[...])
    a_hi, a_lo = _split2(a)

    def proj(j):
        return _dot(a_hi, wqkv_ref[:, j * D_GRP:(j + 1) * D_GRP])

    q1_ref[...] = (proj(0) * ATTN_SCALE).astype(BF16)
    k = proj(1)
    k1_ref[...] = k
    kb1_ref[...] = k.astype(BF16)
    v = proj(2)
    v1_ref[...] = v
    vb1_ref[...] = v.astype(BF16)
    q2_ref[...] = (proj(3) * ATTN_SCALE).astype(BF16)
    k = proj(4)
    k2_ref[...] = k
    kb2_ref[...] = k.astype(BF16)
    v = proj(5)
    v2_ref[...] = v
    vb2_ref[...] = v.astype(BF16)

    wf_hi, wf_lo = _split2(wf_ref[...])
    f = _dot(a_hi, wf_hi) + _dot(a_lo, wf_hi) + _dot(a_hi, wf_lo) + bf_ref[...]
    lf = jnp.minimum(f, 0.0) - jnp.log(1.0 + jnp.exp(-jnp.abs(f)))
    lf_ref[...] = lf[:, :N_HEADS]
    lft_ref[0] = lf.T[:N_HEADS, :]

    @pl.when(i % tiles_per_seq == 0)
    def _():
        carry_ref[...] = jnp.zeros_like(carry_ref)

    c = _dot3_left(tri_ref[...], lf) + carry_ref[...]
    carry_ref[...] = c[c.shape[0] - 1:, :]
    c_ref[...] = c[:, :N_HEADS]
    ct_ref[0] = c.T[:N_HEADS, :]


def _inproj(x2d, g_attn, wqkv, wf_pad, bf_pad, seq_len):
    n = x2d.shape[0]
    tm = min(TOKEN_TILE, n)
    assert n % tm == 0 and (seq_len % tm == 0 or tm % seq_len == 0)
    tiles_per_seq = max(seq_len // tm, 1)
    n_tiles = n // tm
    tri = jnp.tri(tm, dtype=BF16)

    row = lambda w: pl.BlockSpec((tm, w), lambda i: (i, 0))
    const = lambda a: pl.BlockSpec(a.shape, lambda i: (0,) * a.ndim)
    tposed = pl.BlockSpec((1, N_HEADS, tm), lambda i: (i, 0, 0))
    f32_slab = jax.ShapeDtypeStruct((n, D_GRP), F32)
    bf_slab = jax.ShapeDtypeStruct((n, D_GRP), BF16)
    heads = jax.ShapeDtypeStruct((n, N_HEADS), F32)
    heads_t = jax.ShapeDtypeStruct((n_tiles, N_HEADS, tm), F32)
    return pl.pallas_call(
        functools.partial(_inproj_kernel, tiles_per_seq=tiles_per_seq),
        grid=(n_tiles,),
        in_specs=[row(D_MODEL), const(g_attn), const(wqkv), const(wf_pad), const(bf_pad), const(tri)],
        out_specs=[row(D_GRP)] * 6 + [row(N_HEADS)] + [row(D_GRP)] * 4 + [row(N_HEADS), tposed, tposed],
        out_shape=[bf_slab, f32_slab, f32_slab, bf_slab, f32_slab, f32_slab, heads,
                   bf_slab, bf_slab, bf_slab, bf_slab, heads, heads_t, heads_t],
        scratch_shapes=[pltpu.VMEM((1, LANES), F32)],
        compiler_params=_params("arbitrary"),
        name="inproj",
    )(x2d, g_attn, wqkv, wf_pad, bf_pad, tri)


def _sb_step(q, k, v, u, r, acc, mask):
    z = _dot_nt(q, k)
    sp = _softplus(z)
    if mask is not None:
        sp = jnp.where(mask, sp, 0.0)
    hi, lo = _split2(sp)
    c = _dot(hi, u) + _dot(lo, u)
    t = z - c - r
    if mask is not None:
        t = jnp.where(mask, t, NEG_BIG)
    w = jnp.exp(t).astype(BF16)
    return r + c[:, 0:1], acc + _dot(w, v)


def _fox_step(q, k, v, cq, ck, m, l, acc, mask):
    s = _dot_nt(q, k) + cq - ck
    if mask is not None:
        s = jnp.where(mask, s, NEG_BIG)
    m_new = jnp.maximum(m, jnp.max(s, axis=-1, keepdims=True))
    alpha = jnp.exp(m - m_new)
    p = jnp.exp(s - m_new)
    l = alpha * l + jnp.sum(p, axis=-1, keepdims=True)
    acc = alpha * acc + _dot(p.astype(BF16), v)
    return m_new, l, acc


def _head_norm_pair(o, g, even):
    sq = o * o
    s_e = jnp.sum(jnp.where(even, sq, 0.0), axis=-1, keepdims=True)
    s_o = jnp.sum(jnp.where(even, 0.0, sq), axis=-1, keepdims=True)
    inv = jnp.where(even, lax.rsqrt(s_e / HEAD_DIM + EPS), lax.rsqrt(s_o / HEAD_DIM + EPS))
    return o * inv * g


def _pair_masks(rows):
    lane = lax.broadcasted_iota(I32, (rows, LANES), 1)
    return lane < HEAD_DIM


def _split_pair(qp, even):
    zero = jnp.zeros_like(qp)
    return jnp.where(even, qp, zero), jnp.where(even, zero, qp)


def _attn_prompt_kernel(q1_ref, q2_ref, kb1_ref, vb1_ref, kb2_ref, vb2_ref, c_ref, ct_ref,
                        x_ref, gsb_ref, gfox_ref, wout_ref, u_ref, h_ref, y_scr):
    tq = q1_ref.shape[0]
    qi = pl.program_id(1)
    even = _pair_masks(tq)
    rows = lax.broadcasted_iota(I32, (tq, tq), 0)
    cols = lax.broadcasted_iota(I32, (tq, tq), 1)
    strict = cols < rows
    causal = cols <= rows
    u = u_ref[...]
    zero_col = jnp.zeros((tq, 1), F32)
    zero_acc = jnp.zeros((tq, LANES), F32)

    def key_start(kb):
        return pl.multiple_of(kb * tq, tq)

    for p in range(N_PAIRS):
        sl = slice(p * LANES, (p + 1) * LANES)
        q_e, q_o = _split_pair(q1_ref[:, sl], even)

        def sb_block(kb, carry, mask, sl=sl, q_e=q_e, q_o=q_o):
            r_e, a_e, r_o, a_o = carry
            ks = key_start(kb)
            k = kb1_ref[pl.ds(ks, tq), sl]
            v = vb1_ref[pl.ds(ks, tq), sl]
            r_e, a_e = _sb_step(q_e, k, v, u, r_e, a_e, mask)
            r_o, a_o = _sb_step(q_o, k, v, u, r_o, a_o, mask)
            return r_e, a_e, r_o, a_o

        carry = sb_block(qi, (zero_col, zero_acc, zero_col, zero_acc), strict)
        carry = lax.fori_loop(0, qi, lambda j, c, f=sb_block: f(qi - 1 - j, c, None), carry)
        o = jnp.where(even, carry[1], carry[3])
        y_scr[:, sl] = _head_norm_pair(o, gsb_ref[:, sl], even).astype(BF16)

    neg_col = jnp.full((tq, 1), NEG_BIG, F32)
    for p in range(N_PAIRS):
        sl = slice(p * LANES, (p + 1) * LANES)
        q_e, q_o = _split_pair(q2_ref[:, sl], even)
        cq_e = c_ref[:, 2 * p:2 * p + 1]
        cq_o = c_ref[:, 2 * p + 1:2 * p + 2]

        def fox_block(kb, carry, mask, sl=sl, p=p, q_e=q_e, q_o=q_o, cq_e=cq_e, cq_o=cq_o):
            m_e, l_e, a_e, m_o, l_o, a_o = carry
            ks = key_start(kb)
            k = kb2_ref[pl.ds(ks, tq), sl]
            v = vb2_ref[pl.ds(ks, tq), sl]
            ck_e = ct_ref[0, 2 * p:2 * p + 1, pl.ds(ks, tq)]
            ck_o = ct_ref[0, 2 * p + 1:2 * p + 2, pl.ds(ks, tq)]
            m_e, l_e, a_e = _fox_step(q_e, k, v, cq_e, ck_e, m_e, l_e, a_e, mask)
            m_o, l_o, a_o = _fox_step(q_o, k, v, cq_o, ck_o, m_o, l_o, a_o, mask)
            return m_e, l_e, a_e, m_o, l_o, a_o

        carry = fox_block(qi, (neg_col, zero_col, zero_acc, neg_col, zero_col, zero_acc), causal)
        carry = lax.fori_loop(0, qi, lambda j, c, f=fox_block: f(qi - 1 - j, c, None), carry)
        o = jnp.where(even, carry[2] / carry[1], carry[5] / carry[4])
        y_scr[:, D_GRP + p * LANES:D_GRP + (p + 1) * LANES] = (
            _head_norm_pair(o, gfox_ref[:, sl], even).astype(BF16))

    h_ref[...] = x_ref[...] + _dot(y_scr[...], wout_ref[...])


def _attn_prompt(q1, q2, kb1, vb1, kb2, vb2, c, ct, x2d, g_sb, g_fox, wout, batch, seq_len):
    n = x2d.shape[0]
    tq = min(ATTN_TILE, seq_len)
    assert seq_len % tq == 0
    nq = seq_len // tq
    u = jnp.tri(tq, dtype=BF16)

    qblk = lambda w: pl.BlockSpec((tq, w), lambda b, i: (b * nq + i, 0))
    seq = lambda w: pl.BlockSpec((seq_len, w), lambda b, i: (b, 0))
    const = lambda a: pl.BlockSpec(a.shape, lambda b, i: (0,) * a.ndim)
    ct3 = ct.reshape(batch, seq_len // ct.shape[2], N_HEADS, ct.shape[2])
    ct3 = jnp.swapaxes(ct3, 1, 2).reshape(batch, N_HEADS, seq_len)
    return pl.pallas_call(
        _attn_prompt_kernel,
        grid=(batch, nq),
        in_specs=[qblk(D_GRP), qblk(D_GRP), seq(D_GRP), seq(D_GRP), seq(D_GRP), seq(D_GRP),
                  qblk(N_HEADS), pl.BlockSpec((1, N_HEADS, seq_len), lambda b, i: (b, 0, 0)),
                  qblk(D_MODEL), const(g_sb), const(g_fox), const(wout), const(u)],
        out_specs=qblk(D_MODEL),
        out_shape=jax.ShapeDtypeStruct((n, D_MODEL), F32),
        scratch_shapes=[pltpu.VMEM((tq, D_MODEL), BF16)],
        compiler_params=_params("arbitrary", "arbitrary"),
        name="attn_prompt",
    )(q1, q2, kb1, vb1, kb2, vb2, c, ct3, x2d, g_sb, g_fox, wout, u)


def _attn_sample_kernel(q1_ref, q2_ref, kn1_ref, vn1_ref, kn2_ref, vn2_ref,
                        kc1_ref, vc1_ref, kc2_ref, vc2_ref,
                        clf_ref, clft_ref, lfn_ref, lfnt_ref,
                        x_ref, gsb_ref, gfox_ref, wout_ref, u_ref, upast_ref, h_ref, y_scr):
    tq = q1_ref.shape[0]
    past = kc1_ref.shape[0]
    tk = u_ref.shape[0]
    n_cache_blocks = past // tk
    even = _pair_masks(tq)
    rows = lax.broadcasted_iota(I32, (tq, NEW_KEY_PAD), 0)
    cols = lax.broadcasted_iota(I32, (tq, NEW_KEY_PAD), 1)
    strict = cols < rows
    causal = cols <= rows
    u = u_ref[...]
    u_new = u[:NEW_KEY_PAD, :NEW_KEY_PAD]
    zero_col = jnp.zeros((tq, 1), F32)
    zero_acc = jnp.zeros((tq, LANES), F32)
    key_pad = jnp.zeros((NEW_KEY_PAD - tq, LANES), BF16)

    def pad_keys(x):
        return jnp.concatenate([x, key_pad], axis=0)

    ck_cache = _dot3_right(clft_ref[0], upast_ref[...])
    total_row = ck_cache[:, past - 1:past]
    lane = lax.broadcasted_iota(I32, (N_HEADS, LANES), 1)
    run = lfnt_ref[0]
    shift = 1
    while shift < tq:
        run = run + jnp.where(lane >= shift, pltpu.roll(run, shift, axis=1), 0.0)
        shift *= 2
    ck_new = total_row + run
    lfn = lfn_ref[...]
    qrow = lax.broadcasted_iota(I32, (tq, N_HEADS), 0)
    cq = jnp.sum(clf_ref[...], axis=0, keepdims=True) + jnp.zeros((tq, N_HEADS), F32)
    for t in range(tq):
        cq = cq + jnp.where(qrow >= t, lfn[t:t + 1, :], 0.0)

    for p in range(N_PAIRS):
        sl = slice(p * LANES, (p + 1) * LANES)
        q_e, q_o = _split_pair(q1_ref[:, sl], even)
        k = pad_keys(kn1_ref[:, sl])
        v = pad_keys(vn1_ref[:, sl])
        r_e, a_e = _sb_step(q_e, k, v, u_new, zero_col, zero_acc, strict)
        r_o, a_o = _sb_step(q_o, k, v, u_new, zero_col, zero_acc, strict)
        for kb in reversed(range(n_cache_blocks)):
            k = kc1_ref[kb * tk:(kb + 1) * tk, sl].astype(BF16)
            v = vc1_ref[kb * tk:(kb + 1) * tk, sl].astype(BF16)
            r_e, a_e = _sb_step(q_e, k, v, u, r_e, a_e, None)
            r_o, a_o = _sb_step(q_o, k, v, u, r_o, a_o, None)
        o = jnp.where(even, a_e, a_o)
        y_scr[:, sl] = _head_norm_pair(o, gsb_ref[:, sl], even).astype(BF16)

    neg_col = jnp.full((tq, 1), NEG_BIG, F32)
    for p in range(N_PAIRS):
        sl = slice(p * LANES, (p + 1) * LANES)
        q_e, q_o = _split_pair(q2_ref[:, sl], even)
        he, ho = 2 * p, 2 * p + 1
        cq_e, cq_o = cq[:, he:he + 1], cq[:, ho:ho + 1]
        k = pad_keys(kn2_ref[:, sl])
        v = pad_keys(vn2_ref[:, sl])
        m_e, l_e, a_e = _fox_step(q_e, k, v, cq_e, ck_new[he:he + 1, :], neg_col, zero_col, zero_acc, causal)
        m_o, l_o, a_o = _fox_step(q_o, k, v, cq_o, ck_new[ho:ho + 1, :], neg_col, zero_col, zero_acc, causal)
        for kb in reversed(range(n_cache_blocks)):
            ks = slice(kb * tk, (kb + 1) * tk)
            k = kc2_ref[ks, sl].astype(BF16)
            v = vc2_ref[ks, sl].astype(BF16)
            m_e, l_e, a_e = _fox_step(q_e, k, v, cq_e, ck_cache[he:he + 1, ks], m_e, l_e, a_e, None)
            m_o, l_o, a_o = _fox_step(q_o, k, v, cq_o, ck_cache[ho:ho + 1, ks], m_o, l_o, a_o, None)
        o = jnp.where(even, a_e / l_e, a_o / l_o)
        y_scr[:, D_GRP + p * LANES:D_GRP + (p + 1) * LANES] = (
            _head_norm_pair(o, gfox_ref[:, sl], even).astype(BF16))

    h_ref[...] = x_ref[...] + _dot(y_scr[...], wout_ref[...])


def _attn_sample(q1, q2, kn1, vn1, kn2, vn2, kc1, vc1, kc2, vc2, clf, clft, lfn, lfnt,
                 x2d, g_sb, g_fox, wout, batch, t_new, past):
    n = x2d.shape[0]
    tk = min(ATTN_TILE, past)
    assert past % tk == 0 and t_new <= NEW_KEY_PAD <= tk
    u = jnp.tri(tk, dtype=BF16)
    u_past = jnp.tri(past, dtype=BF16).T

    new = lambda w: pl.BlockSpec((t_new, w), lambda b: (b, 0))
    cache = lambda w: pl.BlockSpec((past, w), lambda b: (b, 0))
    const = lambda a: pl.BlockSpec(a.shape, lambda b: (0,) * a.ndim)
    return pl.pallas_call(
        _attn_sample_kernel,
        grid=(batch,),
        in_specs=[new(D_GRP)] * 6 + [cache(D_GRP)] * 4
                 + [cache(N_HEADS), pl.BlockSpec((1, N_HEADS, past), lambda b: (b, 0, 0)),
                    new(N_HEADS), pl.BlockSpec((1, N_HEADS, LANES), lambda b: (b, 0, 0)),
                    new(D_MODEL), const(g_sb), const(g_fox), const(wout), const(u), const(u_past)],
        out_specs=new(D_MODEL),
        out_shape=jax.ShapeDtypeStruct((n, D_MODEL), F32),
        scratch_shapes=[pltpu.VMEM((t_new, D_MODEL), BF16)],
        compiler_params=_params("arbitrary"),
        name="attn_sample",
    )(q1, q2, kn1, vn1, kn2, vn2, kc1, vc1, kc2, vc2, clf, clft, lfn, lfnt,
      x2d, g_sb, g_fox, wout, u, u_past)


def _store_token_tiles(ref, x):
    rows = x.shape[0]
    for j in range(ROW_TILES):
        ref[pl.ds(j, rows, stride=ROW_TILES), :] = x[:, j * LANES:(j + 1) * LANES]


def _load_token_tiles(ref, rows):
    return jnp.concatenate([ref[pl.ds(j, rows, stride=ROW_TILES), :] for j in range(ROW_TILES)], axis=-1)


def _router_kernel(h_ref, g_ref, wr_ref, br_ref, xn_ref, idx_ref, gate_ref):
    tm = h_ref.shape[0]
    xn = _rms(h_ref[...], g_ref[...])
    _store_token_tiles(xn_ref, xn)
    x_hi, x_lo = _split2(xn)
    w_hi, w_lo = _split2(wr_ref[...])
    logits = _dot(x_hi, w_hi) + _dot(x_lo, w_hi) + _dot(x_hi, w_lo) + br_ref[...]
    lane = lax.broadcasted_iota(I32, (tm, LANES), 1).astype(F32)
    vals, idxs = [], []
    for _ in range(TOP_K):
        m = jnp.max(logits, axis=-1, keepdims=True)
        am = jnp.min(jnp.where(logits == m, lane, float(LANES)), axis=-1, keepdims=True)
        vals.append(m)
        idxs.append(am)
        logits = jnp.where(lane == am, -jnp.inf, logits)
    exps = [jnp.exp(v - vals[0]) for v in vals]
    denom = exps[0] + exps[1] + exps[2] + exps[3]
    idx_out = jnp.zeros((tm, LANES), F32)
    gate_out = jnp.zeros((tm, LANES), F32)
    for k in range(TOP_K):
        idx_out = jnp.where(lane == float(k), idxs[k], idx_out)
        gate_out = jnp.where(lane == float(k), exps[k] / denom, gate_out)
    idx_ref[...] = idx_out[:, :TOP_K].astype(I32)
    gate_ref[...] = gate_out[:, :TOP_K]


def _router(h2d, g_ffn, wr_pad, br_pad):
    n = h2d.shape[0]
    tm = min(TOKEN_TILE, n)
    assert n % tm == 0
    const = lambda a: pl.BlockSpec(a.shape, lambda i: (0,) * a.ndim)
    return pl.pallas_call(
        _router_kernel,
        grid=(n // tm,),
        in_specs=[pl.BlockSpec((tm, D_MODEL), lambda i: (i, 0)), const(g_ffn), const(wr_pad), const(br_pad)],
        out_specs=[pl.BlockSpec((tm * ROW_TILES, LANES), lambda i: (i, 0)),
                   pl.BlockSpec((tm, TOP_K), lambda i: (i, 0)),
                   pl.BlockSpec((tm, TOP_K), lambda i: (i, 0))],
        out_shape=[jax.ShapeDtypeStruct((n * ROW_TILES, LANES), F32),
                   jax.ShapeDtypeStruct((n, TOP_K), I32),
                   jax.ShapeDtypeStruct((n, TOP_K), F32)],
        compiler_params=_params("arbitrary"),
        name="router",
    )(h2d, g_ffn, wr_pad, br_pad)


def _moe_kernel(block_e_ref, nvalid_ref, nact_ref, src0_ref, src1_ref, dst_ref,
                x_hbm, wup_ref, bup_ref, wdn_ref, bdn_ref, y_hbm,
                xbuf, ybuf, gsem, ssem, *, rows):
    i = pl.program_id(0)
    nact = nact_ref[0]
    slot = i % 2
    tile_rows = rows * ROW_TILES
    issue_unroll = 8

    def gather_start(src_ref, s):
        def body(r, carry):
            tok = pl.multiple_of(src_ref[0, 0, r] * ROW_TILES, ROW_TILES)
            pltpu.make_async_copy(x_hbm.at[pl.ds(tok, ROW_TILES), :],
                                  xbuf.at[s, pl.ds(r * ROW_TILES, ROW_TILES), :],
                                  gsem.at[s]).start()
            return carry
        lax.fori_loop(0, rows, body, 0, unroll=8)

    def gather_wait(s):
        pltpu.make_async_copy(x_hbm.at[pl.ds(0, tile_rows), :], xbuf.at[s], gsem.at[s]).wait()

    def scatter_start(s, nv):
        def one(r):
            dst = pl.multiple_of(dst_ref[0, 0, r] * ROW_TILES, ROW_TILES)
            pltpu.make_async_copy(ybuf.at[s, pl.ds(r * ROW_TILES, ROW_TILES), :],
                                  y_hbm.at[pl.ds(dst, ROW_TILES), :],
                                  ssem.at[s]).start()

        def group(g, carry):
            for j in range(issue_unroll):
                one(g * issue_unroll + j)
            return carry

        def single(r, carry):
            one(r)
            return carry

        full = nv // issue_unroll
        lax.fori_loop(0, full, group, 0)
        lax.fori_loop(full * issue_unroll, nv, single, 0)

    def scatter_wait(s, nv):
        n = nv * ROW_TILES
        pltpu.make_async_copy(ybuf.at[s, pl.ds(0, n), :], y_hbm.at[pl.ds(0, n), :], ssem.at[s]).wait()

    @pl.when(i < nact)
    def _():
        @pl.when(i == 0)
        def _():
            gather_start(src0_ref, 0)

        @pl.when(i + 1 < nact)
        def _():
            gather_start(src1_ref, 1 - slot)

        gather_wait(slot)
        x = _load_token_tiles(xbuf.at[slot], rows).astype(BF16)
        gu = _dot(x, wup_ref[0]) + bup_ref[0]
        gate = jnp.minimum(gu[:, :D_EXPERT], SWIGLU_LIMIT)
        up = jnp.clip(gu[:, D_EXPERT:], -SWIGLU_LIMIT, SWIGLU_LIMIT)
        hb = (up + 1.0) * gate * (1.0 / (1.0 + jnp.exp(-SWIGLU_ALPHA * gate)))
        y = _dot(hb.astype(BF16), wdn_ref[0]) + bdn_ref[0]

        @pl.when(i >= 2)
        def _():
            scatter_wait(slot, nvalid_ref[i - 2])

        _store_token_tiles(ybuf.at[slot], y)
        scatter_start(slot, nvalid_ref[i])

        @pl.when(i == nact - 1)
        def _():
            @pl.when(i >= 1)
            def _():
                scatter_wait(1 - slot, nvalid_ref[i - 1])
            scatter_wait(slot, nvalid_ref[i])


def _moe(xn_tiles, idx, wup, bup, wdn, bdn, n):
    nk = n * TOP_K
    rows = min(MOE_ROWS, n)
    n_blocks = -(-nk // rows) + N_EXPERTS
    cap = n_blocks * rows

    flat_e = idx.reshape(-1)
    order = jnp.argsort(flat_e, stable=True).astype(I32)
    counts = jnp.sum(flat_e[:, None] == jnp.arange(N_EXPERTS, dtype=I32)[None, :], axis=0, dtype=I32)
    group_start = jnp.cumsum(counts) - counts
    padded = (counts + rows - 1) // rows * rows
    padded_end = jnp.cumsum(padded)
    padded_start = padded_end - padded
    nact = padded_end[-1] // rows
    blk = jnp.arange(n_blocks, dtype=I32)
    blk_e = jnp.minimum(jnp.searchsorted(padded_end, jnp.minimum(blk, nact - 1) * rows, side="right"),
                        N_EXPERTS - 1).astype(I32)
    slot = jnp.arange(cap, dtype=I32)
    slot_e = jnp.repeat(blk_e, rows)
    rank = slot - padded_start[slot_e]
    valid = (rank < counts[slot_e]) & (slot < padded_end[-1])
    flat = order[jnp.clip(group_start[slot_e] + rank, 0, nk - 1)]
    src = jnp.where(valid, flat // TOP_K, 0).astype(I32)
    dst = jnp.where(valid, (flat % TOP_K) * n + flat // TOP_K, 0).astype(I32)
    nvalid = jnp.sum(valid.reshape(n_blocks, rows), axis=1, dtype=I32)

    idx_block = lambda f: pl.BlockSpec((1, 1, rows), f, memory_space=pltpu.SMEM)
    grid_spec = pltpu.PrefetchScalarGridSpec(
        num_scalar_prefetch=3,
        grid=(n_blocks,),
        in_specs=[idx_block(lambda i, be, nv, na: (i, 0, 0)),
                  idx_block(lambda i, be, nv, na: (jnp.minimum(i + 1, n_blocks - 1), 0, 0)),
                  idx_block(lambda i, be, nv, na: (i, 0, 0)),
                  pl.BlockSpec(memory_space=pl.ANY),
                  pl.BlockSpec((1, D_MODEL, 2 * D_EXPERT), lambda i, be, nv, na: (be[i], 0, 0)),
                  pl.BlockSpec((1, 1, 2 * D_EXPERT), lambda i, be, nv, na: (be[i], 0, 0)),
                  pl.BlockSpec((1, D_EXPERT, D_MODEL), lambda i, be, nv, na: (be[i], 0, 0)),
                  pl.BlockSpec((1, 1, D_MODEL), lambda i, be, nv, na: (be[i], 0, 0))],
        out_specs=pl.BlockSpec(memory_space=pl.ANY),
        scratch_shapes=[pltpu.VMEM((2, rows * ROW_TILES, LANES), F32),
                        pltpu.VMEM((2, rows * ROW_TILES, LANES), F32),
                        pltpu.SemaphoreType.DMA((2,)),
                        pltpu.SemaphoreType.DMA((2,))],
    )
    src3 = src.reshape(n_blocks, 1, rows)
    dst3 = dst.reshape(n_blocks, 1, rows)
    return pl.pallas_call(
        functools.partial(_moe_kernel, rows=rows),
        grid_spec=grid_spec,
        out_shape=jax.ShapeDtypeStruct((nk * ROW_TILES, LANES), F32),
        compiler_params=_params("arbitrary"),
        name="moe",
    )(blk_e, nvalid, nact.reshape(1).astype(I32), src3, src3, dst3, xn_tiles, wup, bup, wdn, bdn)


def _final_kernel(h_ref, y0_ref, y1_ref, y2_ref, y3_ref, gate_ref, p_ref,
                  wg_ref, bg_ref, wp_ref, gf_ref, o_ref):
    tm = h_ref.shape[0]
    gates = gate_ref[...]
    h = h_ref[...]
    for k, y_ref in enumerate((y0_ref, y1_ref, y2_ref, y3_ref)):
        h = h + gates[:, k:k + 1] * _load_token_tiles(y_ref, tm)
    g = 1.0 / (1.0 + jnp.exp(-(_dot(h.astype(BF16), wg_ref[...]) + bg_ref[...])))
    h = h + g * _dot(p_ref[...].astype(BF16), wp_ref[...])
    o_ref[...] = _rms(h, gf_ref[...])


def _final(h2d, y_tiles, gates, p2d, wg, bg, wp, g_final):
    n = h2d.shape[0]
    tm = min(ATTN_TILE, n)
    assert n % tm == 0
    nt = n // tm
    const = lambda a: pl.BlockSpec(a.shape, lambda i: (0,) * a.ndim)
    y_spec = lambda k: pl.BlockSpec((tm * ROW_TILES, LANES), lambda i: (k * nt + i, 0))
    return pl.pallas_call(
        _final_kernel,
        grid=(nt,),
        in_specs=[pl.BlockSpec((tm, D_MODEL), lambda i: (i, 0)),
                  y_spec(0), y_spec(1), y_spec(2), y_spec(3),
                  pl.BlockSpec((tm, TOP_K), lambda i: (i, 0)),
                  pl.BlockSpec((tm, D_PLE), lambda i: (i, 0)),
                  const(wg), const(bg), const(wp), const(g_final)],
        out_specs=pl.BlockSpec((tm, D_MODEL), lambda i: (i, 0)),
        out_shape=jax.ShapeDtypeStruct((n, D_MODEL), F32),
        compiler_params=_params("arbitrary"),
        name="final",
    )(h2d, y_tiles, y_tiles, y_tiles, y_tiles, gates, p2d, wg, bg, wp, g_final)


def _pad_lanes(a):
    return jnp.pad(a, ((0, 0), (0, LANES - a.shape[-1])))


def _layer_weights(i, g_attn, w_in, b_f, g_sb, g_fox, w_out, g_ffn, w_router, b_router,
                   w_up, b_up, w_down, b_down, w_ple, w_ple_gate, b_ple_gate):
    n_qkv = 6 * D_GRP
    return dict(
        g_attn=g_attn[i][None, :],
        wqkv=w_in[i][:, :n_qkv].astype(BF16),
        wf=_pad_lanes(w_in[i][:, n_qkv:]),
        bf=_pad_lanes(b_f[i][None, :]),
        g_sb=g_sb[i][None, :], g_fox=g_fox[i][None, :],
        wout=w_out[i].astype(BF16),
        g_ffn=g_ffn[i][None, :],
        wr=_pad_lanes(w_router[i]),
        br=jnp.pad(b_router[i][None, :], ((0, 0), (0, LANES - N_EXPERTS)), constant_values=NEG_BIG),
        wup=w_up[i].astype(BF16), bup=b_up[i][:, None, :],
        wdn=w_down[i].astype(BF16), bdn=b_down[i][:, None, :],
        wg=w_ple_gate[i].astype(BF16), bg=b_ple_gate[i][None, :],
        wp=w_ple[i].astype(BF16),
    )


def _ffn_and_ple(h, p2d, w, g_final_or_none):
    n = h.shape[0]
    xn_tiles, idx, gates = _router(h, w["g_ffn"], w["wr"], w["br"])
    y_tiles = _moe(xn_tiles, idx, w["wup"], w["bup"], w["wdn"], w["bdn"], n)
    return _final(h, y_tiles, gates, p2d, w["wg"], w["bg"], w["wp"], g_final_or_none)


def kernel(x_prompt, x_sample, cache_sb_k, cache_sb_v, cache_fox_k, cache_fox_v, cache_fox_logf,
           p_prompt, p_sample, g_attn, w_in, b_f, g_sb, g_fox, w_out, g_ffn, w_router, b_router,
           w_up, b_up, w_down, b_down, w_ple, w_ple_gate, b_ple_gate, g_final):
    depth = w_in.shape[0]
    assert depth == 1, "the final norm is fused into the layer's last kernel"
    bp, tp, _ = x_prompt.shape
    bs, ts, _ = x_sample.shape
    past = cache_sb_k.shape[2]
    w = _layer_weights(0, g_attn, w_in, b_f, g_sb, g_fox, w_out, g_ffn, w_router, b_router,
                       w_up, b_up, w_down, b_down, w_ple, w_ple_gate, b_ple_gate)
    gfin = g_final[None, :]
    kv5 = lambda a, b, t: a.reshape(1, b, t, N_HEADS, HEAD_DIM)

    xp = x_prompt.reshape(bp * tp, D_MODEL)
    (q1, k1, v1, q2, k2, v2, lf, kb1, vb1, kb2, vb2, c, ct, _) = _inproj(
        xp, w["g_attn"], w["wqkv"], w["wf"], w["bf"], tp)
    hp = _attn_prompt(q1, q2, kb1, vb1, kb2, vb2, c, ct, xp, w["g_sb"], w["g_fox"], w["wout"], bp, tp)
    yp = _ffn_and_ple(hp, p_prompt[0].reshape(bp * tp, D_PLE), w, gfin)
    out_p = (kv5(k1, bp, tp), kv5(v1, bp, tp), kv5(k2, bp, tp), kv5(v2, bp, tp),
             lf.reshape(1, bp, tp, N_HEADS))

    xs = x_sample.reshape(bs * ts, D_MODEL)
    (q1, k1, v1, q2, k2, v2, lf, kb1, vb1, kb2, vb2, _, _, lft) = _inproj(
        xs, w["g_attn"], w["wqkv"], w["wf"], w["bf"], ts)
    cache2d = lambda a: a[0].reshape(bs * past, D_GRP)
    clf = cache_fox_logf[0].astype(F32)
    lfnt = jnp.swapaxes(lft, 0, 1).reshape(N_HEADS, bs, ts)
    lfnt = jnp.pad(jnp.swapaxes(lfnt, 0, 1), ((0, 0), (0, 0), (0, LANES - ts)))
    hs = _attn_sample(q1, q2, kb1, vb1, kb2, vb2,
                      cache2d(cache_sb_k), cache2d(cache_sb_v), cache2d(cache_fox_k), cache2d(cache_fox_v),
                      clf.reshape(bs * past, N_HEADS), jnp.swapaxes(clf, 1, 2), lf, lfnt,
                      xs, w["g_sb"], w["g_fox"], w["wout"], bs, ts, past)
    ys = _ffn_and_ple(hs, p_sample[0].reshape(bs * ts, D_PLE), w, gfin)
    out_s = (kv5(k1, bs, ts), kv5(v1, bs, ts), kv5(k2, bs, ts), kv5(v2, bs, ts),
             lf.reshape(1, bs, ts, N_HEADS))

    return (yp.reshape(bp, tp, D_MODEL), ys.reshape(bs, ts, D_MODEL)) + out_p + out_s
```

```python
import functools

import jax
import jax.numpy as jnp
from jax import lax
from jax.experimental import pallas as pl
from jax.experimental.pallas import tpu as pltpu

F32 = jnp.float32
BF16 = jnp.bfloat16
I32 = jnp.int32

D_MODEL = 1024
N_HEADS = 8
HEAD_DIM = 64
D_GRP = N_HEADS * HEAD_DIM
N_PAIRS = N_HEADS // 2
ATTN_SCALE = HEAD_DIM ** -0.5
LOG2E = 1.4426950408889634
N_EXPERTS = 32
TOP_K = 4
D_EXPERT = D_MODEL
SWIGLU_LIMIT = 7.0
SWIGLU_ALPHA = 1.702
D_PLE = 256
EPS = 1e-6

LANES = 128
SUBLANES = 8
ROW_TILES = D_MODEL // LANES
VMEM_LIMIT_BYTES = 56 * 1024 * 1024
NEG_BIG = -1e30

TOKEN_TILE = 512
ATTN_Q_TILE = 512
ATTN_TILE = 256
NEW_KEY_PAD = 128
MOE_ROWS = 512


def _dot(a, b):
    return jnp.dot(a, b, preferred_element_type=F32)


def _dot_nt(a, b):
    return lax.dot_general(a, b, (((1,), (1,)), ((), ())), preferred_element_type=F32)


def _split2(x):
    hi = x.astype(BF16)
    lo = (x - hi.astype(F32)).astype(BF16)
    return hi, lo


def _split3(x):
    hi = x.astype(BF16)
    r = x - hi.astype(F32)
    mid = r.astype(BF16)
    lo = (r - mid.astype(F32)).astype(BF16)
    return hi, mid, lo


def _dot3_left(m_bf16, x):
    hi, mid, lo = _split3(x)
    return _dot(m_bf16, hi) + _dot(m_bf16, mid) + _dot(m_bf16, lo)


def _dot3_right(x, m_bf16):
    hi, mid, lo = _split3(x)
    return _dot(hi, m_bf16) + _dot(mid, m_bf16) + _dot(lo, m_bf16)


def _rms(x, g):
    return x * lax.rsqrt(jnp.mean(x * x, axis=-1, keepdims=True) + EPS) * g


def _softplus2(z):
    sign_bit = jnp.uint32(0x80000000)
    neg_abs = lax.bitcast_convert_type(lax.bitcast_convert_type(z, jnp.uint32) | sign_bit, F32)
    return jnp.maximum(z, 0.0) + jnp.log2(1.0 + jnp.exp2(neg_abs))


def _params(*sem):
    return pltpu.CompilerParams(dimension_semantics=sem, vmem_limit_bytes=VMEM_LIMIT_BYTES)


def _inproj_kernel(x_ref, g_ref, wqkv_ref, wf_ref, bf_ref, tri_ref,
                   q1_ref, k1_ref, v1_ref, q2_ref, k2_ref, v2_ref, lf_ref,
                   kb1_ref, vb1_ref, kb2_ref, vb2_ref, c_ref, ct_ref, lft_ref,
                   carry_ref, *, tiles_per_seq):
    i = pl.program_id(0)
    a = _rms(x_ref[...], g_ref[...])
    a_hi, a_lo = _split2(a)

    def proj(j):
        return _dot(a_hi, wqkv_ref[:, j * D_GRP:(j + 1) * D_GRP])

    q1_ref[...] = (proj(0) * (ATTN_SCALE * LOG2E)).astype(BF16)
    k = proj(1)
    k1_ref[...] = k
    kb1_ref[...] = k.astype(BF16)
    v = proj(2)
    v1_ref[...] = v
    vb1_ref[...] = v.astype(BF16)
    q2_ref[...] = (proj(3) * (ATTN_SCALE * LOG2E)).astype(BF16)
    k = proj(4)
    k2_ref[...] = k
    kb2_ref[...] = k.astype(BF16)
    v = proj(5)
    v2_ref[...] = v
    vb2_ref[...] = v.astype(BF16)

    wf_hi, wf_lo = _split2(wf_ref[...])
    f = _dot(a_hi, wf_hi) + _dot(a_lo, wf_hi) + _dot(a_hi, wf_lo) + bf_ref[...]
    lf = jnp.minimum(f, 0.0) - jnp.log(1.0 + jnp.exp(-jnp.abs(f)))
    lf_ref[...] = lf[:, :N_HEADS]
    lft_ref[0] = lf.T[:N_HEADS, :]

    @pl.when(i % tiles_per_seq == 0)
    def _():
        carry_ref[...] = jnp.zeros_like(carry_ref)

    c = _dot3_left(tri_ref[...], lf) + carry_ref[...]
    carry_ref[...] = c[c.shape[0] - 1:, :]
    c2 = c * LOG2E
    c_ref[...] = c2[:, :N_HEADS]
    ct_ref[0] = c2.T[:N_HEADS, :]


def _inproj(x2d, g_attn, wqkv, wf_pad, bf_pad, seq_len):
    n = x2d.shape[0]
    tm = min(TOKEN_TILE, n)
    assert n % tm == 0 and (seq_len % tm == 0 or tm % seq_len == 0)
    tiles_per_seq = max(seq_len // tm, 1)
    n_tiles = n // tm
    tri = jnp.tri(tm, dtype=BF16)

    row = lambda w: pl.BlockSpec((tm, w), lambda i: (i, 0))
    const = lambda a: pl.BlockSpec(a.shape, lambda i: (0,) * a.ndim)
    tposed = pl.BlockSpec((1, N_HEADS, tm), lambda i: (i, 0, 0))
    f32_slab = jax.ShapeDtypeStruct((n, D_GRP), F32)
    bf_slab = jax.ShapeDtypeStruct((n, D_GRP), BF16)
    heads = jax.ShapeDtypeStruct((n, N_HEADS), F32)
    heads_t = jax.ShapeDtypeStruct((n_tiles, N_HEADS, tm), F32)
    return pl.pallas_call(
        functools.partial(_inproj_kernel, tiles_per_seq=tiles_per_seq),
        grid=(n_tiles,),
        in_specs=[row(D_MODEL), const(g_attn), const(wqkv), const(wf_pad), const(bf_pad), const(tri)],
        out_specs=[row(D_GRP)] * 6 + [row(N_HEADS)] + [row(D_GRP)] * 4 + [row(N_HEADS), tposed, tposed],
        out_shape=[bf_slab, f32_slab, f32_slab, bf_slab, f32_slab, f32_slab, heads,
                   bf_slab, bf_slab, bf_slab, bf_slab, heads, heads_t, heads_t],
        scratch_shapes=[pltpu.VMEM((1, LANES), F32)],
        compiler_params=_params("arbitrary"),
        name="inproj",
    )(x2d, g_attn, wqkv, wf_pad, bf_pad, tri)


def _sb_step(q, k, v, u, r, acc, mask):
    z = _dot_nt(q, k)
    sp = _softplus2(z)
    if mask is not None:
        sp = jnp.where(mask, sp, 0.0)
    c = _dot(sp.astype(BF16), u)
    t = z - c - r
    if mask is not None:
        t = jnp.where(mask, t, NEG_BIG)
    w = jnp.exp2(t).astype(BF16)
    return r + c[:, 0:1], acc + _dot(w, v)


def _fox_step(q, k, v, cq, ck, m, l, acc, mask):
    s = _dot_nt(q, k) + cq - ck
    if mask is not None:
        s = jnp.where(mask, s, NEG_BIG)
    m_new = jnp.maximum(m, jnp.max(s, axis=-1, keepdims=True))
    alpha = jnp.exp2(m - m_new)
    p = jnp.exp2(s - m_new)
    l = alpha * l + jnp.sum(p, axis=-1, keepdims=True)
    acc = alpha * acc + _dot(p.astype(BF16), v)
    return m_new, l, acc


def _head_norm_pair(o, g, even):
    sq = o * o
    s_e = jnp.sum(jnp.where(even, sq, 0.0), axis=-1, keepdims=True)
    s_o = jnp.sum(jnp.where(even, 0.0, sq), axis=-1, keepdims=True)
    inv = jnp.where(even, lax.rsqrt(s_e / HEAD_DIM + EPS), lax.rsqrt(s_o / HEAD_DIM + EPS))
    return o * inv * g


def _pair_masks(rows):
    lane = lax.broadcasted_iota(I32, (rows, LANES), 1)
    return lane < HEAD_DIM


def _split_pair(qp, even):
    zero = jnp.zeros_like(qp)
    return jnp.where(even, qp, zero), jnp.where(even, zero, qp)


def _attn_prompt_kernel(q1_ref, q2_ref, kb1_ref, vb1_ref, kb2_ref, vb2_ref, c_ref, ct_ref,
                        x_ref, gsb_ref, gfox_ref, wout_ref, u_ref, h_ref, y_scr):
    tq = q1_ref.shape[0]
    tk = u_ref.shape[0]
    ratio = tq // tk
    qi = pl.program_id(1)
    even = _pair_masks(tq)
    rows = lax.broadcasted_iota(I32, (2 * tq, tk), 0)
    rows = jnp.where(rows >= tq, rows - tq, rows)
    cols = lax.broadcasted_iota(I32, (2 * tq, tk), 1)
    row_is_even = lax.broadcasted_iota(I32, (2 * tq, 1), 0) < tq
    u = u_ref[...]
    zero_col = jnp.zeros((2 * tq, 1), F32)
    zero_acc = jnp.zeros((2 * tq, LANES), F32)
    neg_col = jnp.full((2 * tq, 1), NEG_BIG, F32)
    n_full = qi * ratio
    sls = [slice(p * LANES, (p + 1) * LANES) for p in range(N_PAIRS)]

    def stack(qp):
        q_e, q_o = _split_pair(qp, even)
        return jnp.concatenate([q_e, q_o], axis=0)

    def unstack(a):
        return jnp.where(even, a[:tq], a[tq:])

    def key_rows(kb):
        return pl.ds(pl.multiple_of(kb * tk, tk), tk)

    q_sb = [stack(q1_ref[:, sl]) for sl in sls]

    def sb_block(kb, carry, mask):
        ks = key_rows(kb)
        return tuple(_sb_step(q_sb[p], kb1_ref[ks, sls[p]], vb1_ref[ks, sls[p]], u, *carry[p], mask)
                     for p in range(N_PAIRS))

    carry = tuple((zero_col, zero_acc) for _ in range(N_PAIRS))
    for d in range(ratio):
        carry = sb_block(n_full + ratio - 1 - d, carry, cols + (ratio - 1 - d) * tk < rows)
    carry = lax.fori_loop(0, n_full, lambda j, c: sb_block(n_full - 1 - j, c, None), carry)
    for p in range(N_PAIRS):
        y_scr[:, sls[p]] = _head_norm_pair(unstack(carry[p][1]), gsb_ref[:, sls[p]], even).astype(BF16)

    q_fox = [stack(q2_ref[:, sl]) for sl in sls]
    cq = [jnp.concatenate([c_ref[:, 2 * p:2 * p + 1], c_ref[:, 2 * p + 1:2 * p + 2]], axis=0)
          for p in range(N_PAIRS)]

    def fox_block(kb, carry, mask):
        ks = key_rows(kb)
        out = []
        for p in range(N_PAIRS):
            ck = jnp.where(row_is_even, ct_ref[0, 2 * p:2 * p + 1, ks], ct_ref[0, 2 * p + 1:2 * p + 2, ks])
            out.append(_fox_step(q_fox[p], kb2_ref[ks, sls[p]], vb2_ref[ks, sls[p]], cq[p], ck,
                                 *carry[p], mask))
        return tuple(out)

    carry = tuple((neg_col, zero_col, zero_acc) for _ in range(N_PAIRS))
    for d in range(ratio):
        carry = fox_block(n_full + ratio - 1 - d, carry, cols + (ratio - 1 - d) * tk <= rows)
    carry = lax.fori_loop(0, n_full, lambda j, c: fox_block(n_full - 1 - j, c, None), carry)
    for p in range(N_PAIRS):
        o = unstack(carry[p][2] / carry[p][1])
        y_scr[:, D_GRP + p * LANES:D_GRP + (p + 1) * LANES] = (
            _head_norm_pair(o, gfox_ref[:, sls[p]], even).astype(BF16))

    h_ref[...] = x_ref[...] + _dot(y_scr[...], wout_ref[...])


def _attn_prompt(q1, q2, kb1, vb1, kb2, vb2, c, ct, x2d, g_sb, g_fox, wout, batch, seq_len):
    n = x2d.shape[0]
    tq = min(ATTN_Q_TILE, seq_len)
    tk = min(ATTN_TILE, tq)
    assert seq_len % tq == 0 and tq % tk == 0
    nq = seq_len // tq
    u = jnp.tri(tk, dtype=BF16)

    qblk = lambda w: pl.BlockSpec((tq, w), lambda b, i: (b * nq + i, 0))
    seq = lambda w: pl.BlockSpec((seq_len, w), lambda b, i: (b, 0))
    const = lambda a: pl.BlockSpec(a.shape, lambda b, i: (0,) * a.ndim)
    ct3 = ct.reshape(batch, seq_len // ct.shape[2], N_HEADS, ct.shape[2])
    ct3 = jnp.swapaxes(ct3, 1, 2).reshape(batch, N_HEADS, seq_len)
    return pl.pallas_call(
        _attn_prompt_kernel,
        grid=(batch, nq),
        in_specs=[qblk(D_GRP), qblk(D_GRP), seq(D_GRP), seq(D_GRP), seq(D_GRP), seq(D_GRP),
                  qblk(N_HEADS), pl.BlockSpec((1, N_HEADS, seq_len), lambda b, i: (b, 0, 0)),
                  qblk(D_MODEL), const(g_sb), const(g_fox), const(wout), const(u)],
        out_specs=qblk(D_MODEL),
        out_shape=jax.ShapeDtypeStruct((n, D_MODEL), F32),
        scratch_shapes=[pltpu.VMEM((tq, D_MODEL), BF16)],
        compiler_params=_params("arbitrary", "arbitrary"),
        name="attn_prompt",
    )(q1, q2, kb1, vb1, kb2, vb2, c, ct3, x2d, g_sb, g_fox, wout, u)


def _attn_sample_kernel(q1_ref, q2_ref, kn1_ref, vn1_ref, kn2_ref, vn2_ref,
                        kc1_ref, vc1_ref, kc2_ref, vc2_ref,
                        clf_ref, clft_ref, lfn_ref, lfnt_ref,
                        x_ref, gsb_ref, gfox_ref, wout_ref, u_ref, upast_ref, h_ref, y_scr):
    tq = q1_ref.shape[0]
    past = kc1_ref.shape[0]
    tk = u_ref.shape[0]
    n_cache_blocks = past // tk
    even = _pair_masks(tq)
    rows = lax.broadcasted_iota(I32, (tq, NEW_KEY_PAD), 0)
    cols = lax.broadcasted_iota(I32, (tq, NEW_KEY_PAD), 1)
    strict = cols < rows
    causal = cols <= rows
    u = u_ref[...]
    u_new = u[:NEW_KEY_PAD, :NEW_KEY_PAD]
    zero_col = jnp.zeros((tq, 1), F32)
    zero_acc = jnp.zeros((tq, LANES), F32)
    key_pad = jnp.zeros((NEW_KEY_PAD - tq, LANES), BF16)

    def pad_keys(x):
        return jnp.concatenate([x, key_pad], axis=0)

    ck_cache = _dot3_right(clft_ref[0], upast_ref[...])
    total_row = ck_cache[:, past - 1:past]
    ck_cache = ck_cache * LOG2E
    lane = lax.broadcasted_iota(I32, (N_HEADS, LANES), 1)
    run = lfnt_ref[0]
    shift = 1
    while shift < tq:
        run = run + jnp.where(lane >= shift, pltpu.roll(run, shift, axis=1), 0.0)
        shift *= 2
    ck_new = (total_row + run) * LOG2E
    lfn = lfn_ref[...]
    qrow = lax.broadcasted_iota(I32, (tq, N_HEADS), 0)
    cq = jnp.sum(clf_ref[...], axis=0, keepdims=True) + jnp.zeros((tq, N_HEADS), F32)
    for t in range(tq):
        cq = cq + jnp.where(qrow >= t, lfn[t:t + 1, :], 0.0)
    cq = cq * LOG2E

    for p in range(N_PAIRS):
        sl = slice(p * LANES, (p + 1) * LANES)
        q_e, q_o = _split_pair(q1_ref[:, sl], even)
        k = pad_keys(kn1_ref[:, sl])
        v = pad_keys(vn1_ref[:, sl])
        r_e, a_e = _sb_step(q_e, k, v, u_new, zero_col, zero_acc, strict)
        r_o, a_o = _sb_step(q_o, k, v, u_new, zero_col, zero_acc, strict)
        for kb in reversed(range(n_cache_blocks)):
            k = kc1_ref[kb * tk:(kb + 1) * tk, sl].astype(BF16)
            v = vc1_ref[kb * tk:(kb + 1) * tk, sl].astype(BF16)
            r_e, a_e = _sb_step(q_e, k, v, u, r_e, a_e, None)
            r_o, a_o = _sb_step(q_o, k, v, u, r_o, a_o, None)
        o = jnp.where(even, a_e, a_o)
        y_scr[:, sl] = _head_norm_pair(o, gsb_ref[:, sl], even).astype(BF16)

    neg_col = jnp.full((tq, 1), NEG_BIG, F32)
    for p in range(N_PAIRS):
        sl = slice(p * LANES, (p + 1) * LANES)
        q_e, q_o = _split_pair(q2_ref[:, sl], even)
        he, ho = 2 * p, 2 * p + 1
        cq_e, cq_o = cq[:, he:he + 1], cq[:, ho:ho + 1]
        k = pad_keys(kn2_ref[:, sl])
        v = pad_keys(vn2_ref[:, sl])
        m_e, l_e, a_e = _fox_step(q_e, k, v, cq_e, ck_new[he:he + 1, :], neg_col, zero_col, zero_acc, causal)
        m_o, l_o, a_o = _fox_step(q_o, k, v, cq_o, ck_new[ho:ho + 1, :], neg_col, zero_col, zero_acc, causal)
        for kb in reversed(range(n_cache_blocks)):
            ks = slice(kb * tk, (kb + 1) * tk)
            k = kc2_ref[ks, sl].astype(BF16)
            v = vc2_ref[ks, sl].astype(BF16)
            m_e, l_e, a_e = _fox_step(q_e, k, v, cq_e, ck_cache[he:he + 1, ks], m_e, l_e, a_e, None)
            m_o, l_o, a_o = _fox_step(q_o, k, v, cq_o, ck_cache[ho:ho + 1, ks], m_o, l_o, a_o, None)
        o = jnp.where(even, a_e / l_e, a_o / l_o)
        y_scr[:, D_GRP + p * LANES:D_GRP + (p + 1) * LANES] = (
            _head_norm_pair(o, gfox_ref[:, sl], even).astype(BF16))

    h_ref[...] = x_ref[...] + _dot(y_scr[...], wout_ref[...])


def _attn_sample(q1, q2, kn1, vn1, kn2, vn2, kc1, vc1, kc2, vc2, clf, clft, lfn, lfnt,
                 x2d, g_sb, g_fox, wout, batch, t_new, past):
    n = x2d.shape[0]
    tk = min(ATTN_TILE, past)
    assert past % tk == 0 and t_new <= NEW_KEY_PAD <= tk
    u = jnp.tri(tk, dtype=BF16)
    u_past = jnp.tri(past, dtype=BF16).T

    new = lambda w: pl.BlockSpec((t_new, w), lambda b: (b, 0))
    cache = lambda w: pl.BlockSpec((past, w), lambda b: (b, 0))
    const = lambda a: pl.BlockSpec(a.shape, lambda b: (0,) * a.ndim)
    return pl.pallas_call(
        _attn_sample_kernel,
        grid=(batch,),
        in_specs=[new(D_GRP)] * 6 + [cache(D_GRP)] * 4
                 + [cache(N_HEADS), pl.BlockSpec((1, N_HEADS, past), lambda b: (b, 0, 0)),
                    new(N_HEADS), pl.BlockSpec((1, N_HEADS, LANES), lambda b: (b, 0, 0)),
                    new(D_MODEL), const(g_sb), const(g_fox), const(wout), const(u), const(u_past)],
        out_specs=new(D_MODEL),
        out_shape=jax.ShapeDtypeStruct((n, D_MODEL), F32),
        scratch_shapes=[pltpu.VMEM((t_new, D_MODEL), BF16)],
        compiler_params=_params("arbitrary"),
        name="attn_sample",
    )(q1, q2, kn1, vn1, kn2, vn2, kc1, vc1, kc2, vc2, clf, clft, lfn, lfnt,
      x2d, g_sb, g_fox, wout, u, u_past)


def _store_token_tiles(ref, x):
    rows = x.shape[0]
    for j in range(ROW_TILES):
        ref[pl.ds(j, rows, stride=ROW_TILES), :] = x[:, j * LANES:(j + 1) * LANES]


def _load_token_tiles(ref, rows):
    return jnp.concatenate([ref[pl.ds(j, rows, stride=ROW_TILES), :] for j in range(ROW_TILES)], axis=-1)


def _router_kernel(h_ref, g_ref, wr_ref, br_ref, xn_ref, idx_ref, gate_ref):
    tm = h_ref.shape[0]
    xn = _rms(h_ref[...], g_ref[...])
    _store_token_tiles(xn_ref, xn)
    x_hi, x_lo = _split2(xn)
    w_hi, w_lo = _split2(wr_ref[...])
    logits = _dot(x_hi, w_hi) + _dot(x_lo, w_hi) + _dot(x_hi, w_lo) + br_ref[...]
    lane = lax.broadcasted_iota(I32, (tm, LANES), 1).astype(F32)
    vals, idxs = [], []
    for _ in range(TOP_K):
        m = jnp.max(logits, axis=-1, keepdims=True)
        am = jnp.min(jnp.where(logits == m, lane, float(LANES)), axis=-1, keepdims=True)
        vals.append(m)
        idxs.append(am)
        logits = jnp.where(lane == am, -jnp.inf, logits)
    exps = [jnp.exp(v - vals[0]) for v in vals]
    denom = exps[0] + exps[1] + exps[2] + exps[3]
    idx_out = jnp.zeros((tm, LANES), F32)
    gate_out = jnp.zeros((tm, LANES), F32)
    for k in range(TOP_K):
        idx_out = jnp.where(lane == float(k), idxs[k], idx_out)
        gate_out = jnp.where(lane == float(k), exps[k] / denom, gate_out)
    idx_ref[...] = idx_out[:, :TOP_K].astype(I32)
    gate_ref[...] = gate_out[:, :TOP_K]


def _router(h2d, g_ffn, wr_pad, br_pad):
    n = h2d.shape[0]
    tm = min(TOKEN_TILE, n)
    assert n % tm == 0
    const = lambda a: pl.BlockSpec(a.shape, lambda i: (0,) * a.ndim)
    return pl.pallas_call(
        _router_kernel,
        grid=(n // tm,),
        in_specs=[pl.BlockSpec((tm, D_MODEL), lambda i: (i, 0)), const(g_ffn), const(wr_pad), const(br_pad)],
        out_specs=[pl.BlockSpec((tm * ROW_TILES, LANES), lambda i: (i, 0)),
                   pl.BlockSpec((tm, TOP_K), lambda i: (i, 0)),
                   pl.BlockSpec((tm, TOP_K), lambda i: (i, 0))],
        out_shape=[jax.ShapeDtypeStruct((n * ROW_TILES, LANES), F32),
                   jax.ShapeDtypeStruct((n, TOP_K), I32),
                   jax.ShapeDtypeStruct((n, TOP_K), F32)],
        compiler_params=_params("arbitrary"),
        name="router",
    )(h2d, g_ffn, wr_pad, br_pad)


def _moe_kernel(block_e_ref, nvalid_ref, nact_ref, src0_ref, src1_ref, dst_ref,
                x_hbm, wup_ref, bup_ref, wdn_ref, bdn_ref, y_hbm,
                xbuf, ybuf, gsem, ssem, *, rows):
    i = pl.program_id(0)
    nact = nact_ref[0]
    slot = i % 2
    tile_rows = rows * ROW_TILES
    issue_unroll = 8

    def gather_start(src_ref, s):
        def body(r, carry):
            tok = pl.multiple_of(src_ref[0, 0, r] * ROW_TILES, ROW_TILES)
            pltpu.make_async_copy(x_hbm.at[pl.ds(tok, ROW_TILES), :],
                                  xbuf.at[s, pl.ds(r * ROW_TILES, ROW_TILES), :],
                                  gsem.at[s]).start()
            return carry
        lax.fori_loop(0, rows, body, 0, unroll=8)

    def gather_wait(s):
        pltpu.make_async_copy(x_hbm.at[pl.ds(0, tile_rows), :], xbuf.at[s], gsem.at[s]).wait()

    def scatter_start(s, nv):
        def one(r):
            dst = pl.multiple_of(dst_ref[0, 0, r] * ROW_TILES, ROW_TILES)
            pltpu.make_async_copy(ybuf.at[s, pl.ds(r * ROW_TILES, ROW_TILES), :],
                                  y_hbm.at[pl.ds(dst, ROW_TILES), :],
                                  ssem.at[s]).start()

        def group(g, carry):
            for j in range(issue_unroll):
                one(g * issue_unroll + j)
            return carry

        def single(r, carry):
            one(r)
            return carry

        full = nv // issue_unroll
        lax.fori_loop(0, full, group, 0)
        lax.fori_loop(full * issue_unroll, nv, single, 0)

    def scatter_wait(s, nv):
        n = nv * ROW_TILES
        pltpu.make_async_copy(ybuf.at[s, pl.ds(0, n), :], y_hbm.at[pl.ds(0, n), :], ssem.at[s]).wait()

    @pl.when(i < nact)
    def _():
        @pl.when(i == 0)
        def _():
            gather_start(src0_ref, 0)

        @pl.when(i + 1 < nact)
        def _():
            gather_start(src1_ref, 1 - slot)

        gather_wait(slot)
        x = _load_token_tiles(xbuf.at[slot], rows).astype(BF16)
        gu = _dot(x, wup_ref[0]) + bup_ref[0]
        gate = jnp.minimum(gu[:, :D_EXPERT], SWIGLU_LIMIT)
        up = jnp.clip(gu[:, D_EXPERT:], -SWIGLU_LIMIT, SWIGLU_LIMIT)
        hb = (up + 1.0) * gate * (1.0 / (1.0 + jnp.exp(-SWIGLU_ALPHA * gate)))
        y = _dot(hb.astype(BF16), wdn_ref[0]) + bdn_ref[0]

        @pl.when(i >= 2)
        def _():
            scatter_wait(slot, nvalid_ref[i - 2])

        _store_token_tiles(ybuf.at[slot], y)
        scatter_start(slot, nvalid_ref[i])

        @pl.when(i == nact - 1)
        def _():
            @pl.when(i >= 1)
            def _():
                scatter_wait(1 - slot, nvalid_ref[i - 1])
            scatter_wait(slot, nvalid_ref[i])


def _moe(xn_tiles, idx, wup, bup, wdn, bdn, n):
    nk = n * TOP_K
    rows = min(MOE_ROWS, n)
    n_blocks = -(-nk // rows) + N_EXPERTS
    cap = n_blocks * rows

    flat_e = idx.reshape(-1)
    order = jnp.argsort(flat_e, stable=True).astype(I32)
    counts = jnp.sum(flat_e[:, None] == jnp.arange(N_EXPERTS, dtype=I32)[None, :], axis=0, dtype=I32)
    group_start = jnp.cumsum(counts) - counts
    padded = (counts + rows - 1) // rows * rows
    padded_end = jnp.cumsum(padded)
    padded_start = padded_end - padded
    nact = padded_end[-1] // rows
    blk = jnp.arange(n_blocks, dtype=I32)
    blk_e = jnp.minimum(jnp.searchsorted(padded_end, jnp.minimum(blk, nact - 1) * rows, side="right"),
                        N_EXPERTS - 1).astype(I32)
    slot = jnp.arange(cap, dtype=I32)
    slot_e = jnp.repeat(blk_e, rows)
    rank = slot - padded_start[slot_e]
    valid = (rank < counts[slot_e]) & (slot < padded_end[-1])
    flat = order[jnp.clip(group_start[slot_e] + rank, 0, nk - 1)]
    src = jnp.where(valid, flat // TOP_K, 0).astype(I32)
    dst = jnp.where(valid, (flat % TOP_K) * n + flat // TOP_K, 0).astype(I32)
    nvalid = jnp.sum(valid.reshape(n_blocks, rows), axis=1, dtype=I32)

    idx_block = lambda f: pl.BlockSpec((1, 1, rows), f, memory_space=pltpu.SMEM)
    grid_spec = pltpu.PrefetchScalarGridSpec(
        num_scalar_prefetch=3,
        grid=(n_blocks,),
        in_specs=[idx_block(lambda i, be, nv, na: (i, 0, 0)),
                  idx_block(lambda i, be, nv, na: (jnp.minimum(i + 1, n_blocks - 1), 0, 0)),
                  idx_block(lambda i, be, nv, na: (i, 0, 0)),
                  pl.BlockSpec(memory_space=pl.ANY),
                  pl.BlockSpec((1, D_MODEL, 2 * D_EXPERT), lambda i, be, nv, na: (be[i], 0, 0)),
                  pl.BlockSpec((1, 1, 2 * D_EXPERT), lambda i, be, nv, na: (be[i], 0, 0)),
                  pl.BlockSpec((1, D_EXPERT, D_MODEL), lambda i, be, nv, na: (be[i], 0, 0)),
                  pl.BlockSpec((1, 1, D_MODEL), lambda i, be, nv, na: (be[i], 0, 0))],
        out_specs=pl.BlockSpec(memory_space=pl.ANY),
        scratch_shapes=[pltpu.VMEM((2, rows * ROW_TILES, LANES), F32),
                        pltpu.VMEM((2, rows * ROW_TILES, LANES), F32),
                        pltpu.SemaphoreType.DMA((2,)),
                        pltpu.SemaphoreType.DMA((2,))],
    )
    src3 = src.reshape(n_blocks, 1, rows)
    dst3 = dst.reshape(n_blocks, 1, rows)
    return pl.pallas_call(
        functools.partial(_moe_kernel, rows=rows),
        grid_spec=grid_spec,
        out_shape=jax.ShapeDtypeStruct((nk * ROW_TILES, LANES), F32),
        compiler_params=_params("arbitrary"),
        name="moe",
    )(blk_e, nvalid, nact.reshape(1).astype(I32), src3, src3, dst3, xn_tiles, wup, bup, wdn, bdn)


def _final_kernel(h_ref, y0_ref, y1_ref, y2_ref, y3_ref, gate_ref, p_ref,
                  wg_ref, bg_ref, wp_ref, gf_ref, o_ref):
    tm = h_ref.shape[0]
    gates = gate_ref[...]
    h = h_ref[...]
    for k, y_ref in enumerate((y0_ref, y1_ref, y2_ref, y3_ref)):
        h = h + gates[:, k:k + 1] * _load_token_tiles(y_ref, tm)
    g = 1.0 / (1.0 + jnp.exp(-(_dot(h.astype(BF16), wg_ref[...]) + bg_ref[...])))
    h = h + g * _dot(p_ref[...].astype(BF16), wp_ref[...])
    o_ref[...] = _rms(h, gf_ref[...])


def _final(h2d, y_tiles, gates, p2d, wg, bg, wp, g_final):
    n = h2d.shape[0]
    tm = min(ATTN_TILE, n)
    assert n % tm == 0
    nt = n // tm
    const = lambda a: pl.BlockSpec(a.shape, lambda i: (0,) * a.ndim)
    y_spec = lambda k: pl.BlockSpec((tm * ROW_TILES, LANES), lambda i: (k * nt + i, 0))
    return pl.pallas_call(
        _final_kernel,
        grid=(nt,),
        in_specs=[pl.BlockSpec((tm, D_MODEL), lambda i: (i, 0)),
                  y_spec(0), y_spec(1), y_spec(2), y_spec(3),
                  pl.BlockSpec((tm, TOP_K), lambda i: (i, 0)),
                  pl.BlockSpec((tm, D_PLE), lambda i: (i, 0)),
                  const(wg), const(bg), const(wp), const(g_final)],
        out_specs=pl.BlockSpec((tm, D_MODEL), lambda i: (i, 0)),
        out_shape=jax.ShapeDtypeStruct((n, D_MODEL), F32),
        compiler_params=_params("arbitrary"),
        name="final",
    )(h2d, y_tiles, y_tiles, y_tiles, y_tiles, gates, p2d, wg, bg, wp, g_final)


def _pad_lanes(a):
    return jnp.pad(a, ((0, 0), (0, LANES - a.shape[-1])))


def _layer_weights(i, g_attn, w_in, b_f, g_sb, g_fox, w_out, g_ffn, w_router, b_router,
                   w_up, b_up, w_down, b_down, w_ple, w_ple_gate, b_ple_gate):
    n_qkv = 6 * D_GRP
    return dict(
        g_attn=g_attn[i][None, :],
        wqkv=w_in[i][:, :n_qkv].astype(BF16),
        wf=_pad_lanes(w_in[i][:, n_qkv:]),
        bf=_pad_lanes(b_f[i][None, :]),
        g_sb=g_sb[i][None, :], g_fox=g_fox[i][None, :],
        wout=w_out[i].astype(BF16),
        g_ffn=g_ffn[i][None, :],
        wr=_pad_lanes(w_router[i]),
        br=jnp.pad(b_router[i][None, :], ((0, 0), (0, LANES - N_EXPERTS)), constant_values=NEG_BIG),
        wup=w_up[i].astype(BF16), bup=b_up[i][:, None, :],
        wdn=w_down[i].astype(BF16), bdn=b_down[i][:, None, :],
        wg=w_ple_gate[i].astype(BF16), bg=b_ple_gate[i][None, :],
        wp=w_ple[i].astype(BF16),
    )


def _ffn_and_ple(h, p2d, w, g_final_or_none):
    n = h.shape[0]
    xn_tiles, idx, gates = _router(h, w["g_ffn"], w["wr"], w["br"])
    y_tiles = _moe(xn_tiles, idx, w["wup"], w["bup"], w["wdn"], w["bdn"], n)
    return _final(h, y_tiles, gates, p2d, w["wg"], w["bg"], w["wp"], g_final_or_none)


def kernel(x_prompt, x_sample, cache_sb_k, cache_sb_v, cache_fox_k, cache_fox_v, cache_fox_logf,
           p_prompt, p_sample, g_attn, w_in, b_f, g_sb, g_fox, w_out, g_ffn, w_router, b_router,
           w_up, b_up, w_down, b_down, w_ple, w_ple_gate, b_ple_gate, g_final):
    depth = w_in.shape[0]
    assert depth == 1, "the final norm is fused into the layer's last kernel"
    bp, tp, _ = x_prompt.shape
    bs, ts, _ = x_sample.shape
    past = cache_sb_k.shape[2]
    w = _layer_weights(0, g_attn, w_in, b_f, g_sb, g_fox, w_out, g_ffn, w_router, b_router,
                       w_up, b_up, w_down, b_down, w_ple, w_ple_gate, b_ple_gate)
    gfin = g_final[None, :]
    kv5 = lambda a, b, t: a.reshape(1, b, t, N_HEADS, HEAD_DIM)

    xp = x_prompt.reshape(bp * tp, D_MODEL)
    (q1, k1, v1, q2, k2, v2, lf, kb1, vb1, kb2, vb2, c, ct, _) = _inproj(
        xp, w["g_attn"], w["wqkv"], w["wf"], w["bf"], tp)
    hp = _attn_prompt(q1, q2, kb1, vb1, kb2, vb2, c, ct, xp, w["g_sb"], w["g_fox"], w["wout"], bp, tp)
    yp = _ffn_and_ple(hp, p_prompt[0].reshape(bp * tp, D_PLE), w, gfin)
    out_p = (kv5(k1, bp, tp), kv5(v1, bp, tp), kv5(k2, bp, tp), kv5(v2, bp, tp),
             lf.reshape(1, bp, tp, N_HEADS))

    xs = x_sample.reshape(bs * ts, D_MODEL)
    (q1, k1, v1, q2, k2, v2, lf, kb1, vb1, kb2, vb2, _, _, lft) = _inproj(
        xs, w["g_attn"], w["wqkv"], w["wf"], w["bf"], ts)
    cache2d = lambda a: a[0].reshape(bs * past, D_GRP)
    clf = cache_fox_logf[0].astype(F32)
    lfnt = jnp.swapaxes(lft, 0, 1).reshape(N_HEADS, bs, ts)
    lfnt = jnp.pad(jnp.swapaxes(lfnt, 0, 1), ((0, 0), (0, 0), (0, LANES - ts)))
    hs = _attn_sample(q1, q2, kb1, vb1, kb2, vb2,
                      cache2d(cache_sb_k), cache2d(cache_sb_v), cache2d(cache_fox_k), cache2d(cache_fox_v),
                      clf.reshape(bs * past, N_HEADS), jnp.swapaxes(clf, 1, 2), lf, lfnt,
                      xs, w["g_sb"], w["g_fox"], w["wout"], bs, ts, past)
    ys = _ffn_and_ple(hs, p_sample[0].reshape(bs * ts, D_PLE), w, gfin)
    out_s = (kv5(k1, bs, ts), kv5(v1, bs, ts), kv5(k2, bs, ts), kv5(v2, bs, ts),
             lf.reshape(1, bs, ts, N_HEADS))

    return (yp.reshape(bp, tp, D_MODEL), ys.reshape(bs, ts, D_MODEL)) + out_p + out_s
```

```python
import functools

import jax
import jax.numpy as jnp
from jax import lax
from jax.experimental import pallas as pl
from jax.experimental.pallas import tpu as pltpu

F32 = jnp.float32
BF16 = jnp.bfloat16
I32 = jnp.int32

D_MODEL = 1024
N_HEADS = 8
HEAD_DIM = 64
D_GRP = N_HEADS * HEAD_DIM
N_PAIRS = N_HEADS // 2
ATTN_SCALE = HEAD_DIM ** -0.5
LOG2E = 1.4426950408889634
N_EXPERTS = 32
TOP_K = 4
D_EXPERT = D_MODEL
SWIGLU_LIMIT = 7.0
SWIGLU_ALPHA = 1.702
D_PLE = 256
EPS = 1e-6

LANES = 128
SUBLANES = 8
ROW_TILES = D_MODEL // LANES
VMEM_LIMIT_BYTES = 56 * 1024 * 1024
NEG_BIG = -1e30

TOKEN_TILE = 512
ATTN_Q_TILE = 512
ATTN_TILE = 256
NEW_KEY_PAD = 128
MOE_ROWS = 512
MOE_MIN_ROWS = 128


def _dot(a, b):
    return jnp.dot(a, b, preferred_element_type=F32)


def _dot_nt(a, b):
    return lax.dot_general(a, b, (((1,), (1,)), ((), ())), preferred_element_type=F32)


def _split2(x):
    hi = x.astype(BF16)
    lo = (x - hi.astype(F32)).astype(BF16)
    return hi, lo


def _split3(x):
    hi = x.astype(BF16)
    r = x - hi.astype(F32)
    mid = r.astype(BF16)
    lo = (r - mid.astype(F32)).astype(BF16)
    return hi, mid, lo


def _dot3_left(m_bf16, x):
    hi, mid, lo = _split3(x)
    return _dot(m_bf16, hi) + _dot(m_bf16, mid) + _dot(m_bf16, lo)


def _dot3_right(x, m_bf16):
    hi, mid, lo = _split3(x)
    return _dot(hi, m_bf16) + _dot(mid, m_bf16) + _dot(lo, m_bf16)


def _rms(x, g):
    return x * lax.rsqrt(jnp.mean(x * x, axis=-1, keepdims=True) + EPS) * g


def _softplus2(z):
    sign_bit = jnp.uint32(0x80000000)
    neg_abs = lax.bitcast_convert_type(lax.bitcast_convert_type(z, jnp.uint32) | sign_bit, F32)
    return jnp.maximum(z, 0.0) + jnp.log2(1.0 + jnp.exp2(neg_abs))


def _params(*sem):
    return pltpu.CompilerParams(dimension_semantics=sem, vmem_limit_bytes=VMEM_LIMIT_BYTES)


def _inproj_common(x_ref, g_ref, wqkv_ref, wf_ref, bf_ref):
    a = _rms(x_ref[...], g_ref[...])
    a_hi, a_lo = _split2(a)

    def proj(j):
        return _dot(a_hi, wqkv_ref[:, j * D_GRP:(j + 1) * D_GRP])

    wf_hi, wf_lo = _split2(wf_ref[...])
    f = _dot(a_hi, wf_hi) + _dot(a_lo, wf_hi) + _dot(a_hi, wf_lo) + bf_ref[...]
    lf = jnp.minimum(f, 0.0) - jnp.log(1.0 + jnp.exp(-jnp.abs(f)))
    return proj, lf


def _inproj_rows_kernel(x_ref, g_ref, wqkv_ref, wf_ref, bf_ref,
                        q1_ref, k1_ref, v1_ref, q2_ref, k2_ref, v2_ref, lf_ref,
                        kb1_ref, vb1_ref, kb2_ref, vb2_ref, lft_ref):
    proj, lf = _inproj_common(x_ref, g_ref, wqkv_ref, wf_ref, bf_ref)
    q1_ref[...] = (proj(0) * (ATTN_SCALE * LOG2E)).astype(BF16)
    q2_ref[...] = (proj(3) * (ATTN_SCALE * LOG2E)).astype(BF16)
    for j, (o_ref, ob_ref) in ((1, (k1_ref, kb1_ref)), (2, (v1_ref, vb1_ref)),
                               (4, (k2_ref, kb2_ref)), (5, (v2_ref, vb2_ref))):
        y = proj(j)
        o_ref[...] = y
        ob_ref[...] = y.astype(BF16)
    lf_ref[...] = lf[:, :N_HEADS]
    lft_ref[0] = lf.T[:N_HEADS, :]


def _inproj_seq_kernel(x_ref, g_ref, wqkv_ref, wf_ref, bf_ref, tri_ref,
                       q1_ref, k1t_ref, v1t_ref, q2_ref, k2t_ref, v2t_ref, lft_ref,
                       kb1t_ref, vb1_ref, kb2t_ref, vb2_ref, c_ref, ct_ref,
                       carry_ref):
    t = pl.program_id(1)
    proj, lf = _inproj_common(x_ref, g_ref, wqkv_ref, wf_ref, bf_ref)
    q1_ref[...] = (proj(0) * (ATTN_SCALE * LOG2E)).astype(BF16)
    q2_ref[...] = (proj(3) * (ATTN_SCALE * LOG2E)).astype(BF16)
    for j, (ot_ref, obt_ref) in ((1, (k1t_ref, kb1t_ref)), (4, (k2t_ref, kb2t_ref))):
        yt = proj(j).T
        ot_ref[0] = yt
        obt_ref[0] = yt.astype(BF16)
    for j, (ot_ref, ob_ref) in ((2, (v1t_ref, vb1_ref)), (5, (v2t_ref, vb2_ref))):
        y = proj(j)
        ot_ref[0] = y.T
        ob_ref[...] = y.astype(BF16)
    lft_ref[0] = lf.T[:N_HEADS, :]

    @pl.when(t == 0)
    def _():
        carry_ref[...] = jnp.zeros_like(carry_ref)

    c = _dot3_left(tri_ref[...], lf) + carry_ref[...]
    carry_ref[...] = c[c.shape[0] - 1:, :]
    c2 = c * LOG2E
    c_ref[...] = c2[:, :N_HEADS]
    ct_ref[0] = c2.T[:N_HEADS, :]


def _inproj_rows(x2d, g_attn, wqkv, wf_pad, bf_pad):
    n = x2d.shape[0]
    tm = min(TOKEN_TILE, n)
    assert n % tm == 0
    n_tiles = n // tm
    row = lambda w: pl.BlockSpec((tm, w), lambda i: (i, 0))
    const = lambda a: pl.BlockSpec(a.shape, lambda i: (0,) * a.ndim)
    f32_slab = jax.ShapeDtypeStruct((n, D_GRP), F32)
    bf_slab = jax.ShapeDtypeStruct((n, D_GRP), BF16)
    return pl.pallas_call(
        _inproj_rows_kernel,
        grid=(n_tiles,),
        in_specs=[row(D_MODEL), const(g_attn), const(wqkv), const(wf_pad), const(bf_pad)],
        out_specs=[row(D_GRP)] * 6 + [row(N_HEADS)] + [row(D_GRP)] * 4
                  + [pl.BlockSpec((1, N_HEADS, tm), lambda i: (i, 0, 0))],
        out_shape=[bf_slab, f32_slab, f32_slab, bf_slab, f32_slab, f32_slab,
                   jax.ShapeDtypeStruct((n, N_HEADS), F32),
                   bf_slab, bf_slab, bf_slab, bf_slab,
                   jax.ShapeDtypeStruct((n_tiles, N_HEADS, tm), F32)],
        compiler_params=_params("arbitrary"),
        name="inproj_rows",
    )(x2d, g_attn, wqkv, wf_pad, bf_pad)


def _inproj_seq(x2d, g_attn, wqkv, wf_pad, bf_pad, batch, seq_len):
    n = x2d.shape[0]
    tm = min(TOKEN_TILE, seq_len)
    assert seq_len % tm == 0
    tps = seq_len // tm
    tri = jnp.tri(tm, dtype=BF16)

    row = lambda w: pl.BlockSpec((tm, w), lambda b, t: (b * tps + t, 0))
    const = lambda a: pl.BlockSpec(a.shape, lambda b, t: (0,) * a.ndim)
    tposed = lambda w: pl.BlockSpec((1, w, tm), lambda b, t: (b, 0, t))
    f32_t = jax.ShapeDtypeStruct((batch, D_GRP, seq_len), F32)
    bf_t = jax.ShapeDtypeStruct((batch, D_GRP, seq_len), BF16)
    bf_slab = jax.ShapeDtypeStruct((n, D_GRP), BF16)
    heads_t = jax.ShapeDtypeStruct((batch, N_HEADS, seq_len), F32)
    return pl.pallas_call(
        _inproj_seq_kernel,
        grid=(batch, tps),
        in_specs=[row(D_MODEL), const(g_attn), const(wqkv), const(wf_pad), const(bf_pad), const(tri)],
        out_specs=[row(D_GRP), tposed(D_GRP), tposed(D_GRP), row(D_GRP), tposed(D_GRP), tposed(D_GRP),
                   tposed(N_HEADS), tposed(D_GRP), row(D_GRP), tposed(D_GRP), row(D_GRP),
                   row(N_HEADS), tposed(N_HEADS)],
        out_shape=[bf_slab, f32_t, f32_t, bf_slab, f32_t, f32_t, heads_t,
                   bf_t, bf_slab, bf_t, bf_slab,
                   jax.ShapeDtypeStruct((n, N_HEADS), F32), heads_t],
        scratch_shapes=[pltpu.VMEM((1, LANES), F32)],
        compiler_params=_params("arbitrary", "arbitrary"),
        name="inproj_seq",
    )(x2d, g_attn, wqkv, wf_pad, bf_pad, tri)


def _scores(q, k, k_feature_major):
    return _dot(q, k) if k_feature_major else _dot_nt(q, k)


def _weighted_values(w, v, v_feature_major):
    return _dot_nt(w, v) if v_feature_major else _dot(w, v)


def _sb_step(q, k, v, u, r, acc, mask, k_feature_major=False, v_feature_major=False):
    z = _scores(q, k, k_feature_major)
    sp = _softplus2(z)
    if mask is not None:
        sp = jnp.where(mask, sp, 0.0)
    c = _dot(sp.astype(BF16), u)
    t = z - c - r
    if mask is not None:
        t = jnp.where(mask, t, NEG_BIG)
    w = jnp.exp2(t).astype(BF16)
    return r + c[:, 0:1], acc + _weighted_values(w, v, v_feature_major)


def _fox_step(q, k, v, cq, ck, m, l, acc, mask, k_feature_major=False, v_feature_major=False):
    s = _scores(q, k, k_feature_major) + cq - ck
    if mask is not None:
        s = jnp.where(mask, s, NEG_BIG)
    m_new = jnp.maximum(m, jnp.max(s, axis=-1, keepdims=True))
    alpha = jnp.exp2(m - m_new)
    p = jnp.exp2(s - m_new)
    l = alpha * l + jnp.sum(p, axis=-1, keepdims=True)
    acc = alpha * acc + _weighted_values(p.astype(BF16), v, v_feature_major)
    return m_new, l, acc


def _head_norm_pair(o, g, even):
    sq = o * o
    s_e = jnp.sum(jnp.where(even, sq, 0.0), axis=-1, keepdims=True)
    s_o = jnp.sum(jnp.where(even, 0.0, sq), axis=-1, keepdims=True)
    inv = jnp.where(even, lax.rsqrt(s_e / HEAD_DIM + EPS), lax.rsqrt(s_o / HEAD_DIM + EPS))
    return o * inv * g


def _pair_masks(rows):
    lane = lax.broadcasted_iota(I32, (rows, LANES), 1)
    return lane < HEAD_DIM


def _split_pair(qp, even):
    zero = jnp.zeros_like(qp)
    return jnp.where(even, qp, zero), jnp.where(even, zero, qp)


def _attn_prompt_kernel(q1_ref, q2_ref, kt1_ref, vb1_ref, kt2_ref, vb2_ref, c_ref, ct_ref,
                        x_ref, gsb_ref, gfox_ref, wout_ref, u_ref, h_ref, y_scr):
    tq = q1_ref.shape[0]
    tk = u_ref.shape[0]
    ratio = tq // tk
    qi = pl.program_id(1)
    even = _pair_masks(tq)
    rows = lax.broadcasted_iota(I32, (2 * tq, tk), 0)
    rows = jnp.where(rows >= tq, rows - tq, rows)
    cols = lax.broadcasted_iota(I32, (2 * tq, tk), 1)
    row_is_even = lax.broadcasted_iota(I32, (2 * tq, 1), 0) < tq
    u = u_ref[...]
    zero_col = jnp.zeros((2 * tq, 1), F32)
    zero_acc = jnp.zeros((2 * tq, LANES), F32)
    neg_col = jnp.full((2 * tq, 1), NEG_BIG, F32)
    n_full = qi * ratio
    sls = [slice(p * LANES, (p + 1) * LANES) for p in range(N_PAIRS)]

    def stack(qp):
        q_e, q_o = _split_pair(qp, even)
        return jnp.concatenate([q_e, q_o], axis=0)

    def unstack(a):
        return jnp.where(even, a[:tq], a[tq:])

    def key_rows(kb):
        return pl.ds(pl.multiple_of(kb * tk, tk), tk)

    q_sb = [stack(q1_ref[:, sl]) for sl in sls]

    def sb_block(kb, carry, mask):
        ks = key_rows(kb)
        return tuple(_sb_step(q_sb[p], kt1_ref[0, sls[p], ks], vb1_ref[ks, sls[p]], u, *carry[p], mask,
                              k_feature_major=True)
                     for p in range(N_PAIRS))

    carry = tuple((zero_col, zero_acc) for _ in range(N_PAIRS))
    for d in range(ratio):
        carry = sb_block(n_full + ratio - 1 - d, carry, cols + (ratio - 1 - d) * tk < rows)
    carry = lax.fori_loop(0, n_full, lambda j, c: sb_block(n_full - 1 - j, c, None), carry)
    for p in range(N_PAIRS):
        y_scr[:, sls[p]] = _head_norm_pair(unstack(carry[p][1]), gsb_ref[:, sls[p]], even).astype(BF16)

    q_fox = [stack(q2_ref[:, sl]) for sl in sls]
    cq = [jnp.concatenate([c_ref[:, 2 * p:2 * p + 1], c_ref[:, 2 * p + 1:2 * p + 2]], axis=0)
          for p in range(N_PAIRS)]

    def fox_block(kb, carry, mask):
        ks = key_rows(kb)
        out = []
        for p in range(N_PAIRS):
            ck = jnp.where(row_is_even, ct_ref[0, 2 * p:2 * p + 1, ks], ct_ref[0, 2 * p + 1:2 * p + 2, ks])
            out.append(_fox_step(q_fox[p], kt2_ref[0, sls[p], ks], vb2_ref[ks, sls[p]], cq[p], ck,
                                 *carry[p], mask, k_feature_major=True))
        return tuple(out)

    carry = tuple((neg_col, zero_col, zero_acc) for _ in range(N_PAIRS))
    for d in range(ratio):
        carry = fox_block(n_full + ratio - 1 - d, carry, cols + (ratio - 1 - d) * tk <= rows)
    carry = lax.fori_loop(0, n_full, lambda j, c: fox_block(n_full - 1 - j, c, None), carry)
    for p in range(N_PAIRS):
        o = unstack(carry[p][2] / carry[p][1])
        y_scr[:, D_GRP + p * LANES:D_GRP + (p + 1) * LANES] = (
            _head_norm_pair(o, gfox_ref[:, sls[p]], even).astype(BF16))

    h_ref[...] = x_ref[...] + _dot(y_scr[...], wout_ref[...])


def _attn_prompt(q1, q2, kt1, vb1, kt2, vb2, c, ct, x2d, g_sb, g_fox, wout, batch, seq_len):
    n = x2d.shape[0]
    tq = min(ATTN_Q_TILE, seq_len)
    tk = min(ATTN_TILE, tq)
    assert seq_len % tq == 0 and tq % tk == 0
    nq = seq_len // tq
    u = jnp.tri(tk, dtype=BF16)

    qblk = lambda w: pl.BlockSpec((tq, w), lambda b, i: (b * nq + i, 0))
    seq = lambda w: pl.BlockSpec((seq_len, w), lambda b, i: (b, 0))
    seq_t = lambda w: pl.BlockSpec((1, w, seq_len), lambda b, i: (b, 0, 0))
    const = lambda a: pl.BlockSpec(a.shape, lambda b, i: (0,) * a.ndim)
    return pl.pallas_call(
        _attn_prompt_kernel,
        grid=(batch, nq),
        in_specs=[qblk(D_GRP), qblk(D_GRP), seq_t(D_GRP), seq(D_GRP), seq_t(D_GRP), seq(D_GRP),
                  qblk(N_HEADS), seq_t(N_HEADS),
                  qblk(D_MODEL), const(g_sb), const(g_fox), const(wout), const(u)],
        out_specs=qblk(D_MODEL),
        out_shape=jax.ShapeDtypeStruct((n, D_MODEL), F32),
        scratch_shapes=[pltpu.VMEM((tq, D_MODEL), BF16)],
        compiler_params=_params("arbitrary", "arbitrary"),
        name="attn_prompt",
    )(q1, q2, kt1, vb1, kt2, vb2, c, ct, x2d, g_sb, g_fox, wout, u)


def _attn_sample_kernel(q1_ref, q2_ref, kn1_ref, vn1_ref, kn2_ref, vn2_ref,
                        kc1_ref, vc1_ref, kc2_ref, vc2_ref, clft_ref, lfnt_ref,
                        x_ref, gsb_ref, gfox_ref, wout_ref, u_ref, upast_ref, h_ref, y_scr):
    tq = q1_ref.shape[0]
    past = kc1_ref.shape[2]
    tk = u_ref.shape[0]
    n_cache_blocks = past // tk
    even = _pair_masks(tq)
    rows = lax.broadcasted_iota(I32, (tq, NEW_KEY_PAD), 0)
    cols = lax.broadcasted_iota(I32, (tq, NEW_KEY_PAD), 1)
    strict = cols < rows
    causal = cols <= rows
    u = u_ref[...]
    u_new = u[:NEW_KEY_PAD, :NEW_KEY_PAD]
    zero_col = jnp.zeros((tq, 1), F32)
    zero_acc = jnp.zeros((tq, LANES), F32)
    key_pad = jnp.zeros((NEW_KEY_PAD - tq, LANES), BF16)

    def pad_keys(x):
        return jnp.concatenate([x, key_pad], axis=0)

    ck_cache = _dot3_right(clft_ref[0], upast_ref[...])
    total_row = ck_cache[:, past - 1:past]
    ck_cache = ck_cache * LOG2E
    lane = lax.broadcasted_iota(I32, (N_HEADS, LANES), 1)
    run = lfnt_ref[0]
    shift = 1
    while shift < tq:
        run = run + jnp.where(lane >= shift, pltpu.roll(run, shift, axis=1), 0.0)
        shift *= 2
    ck_new = (total_row + run) * LOG2E

    def query_col(h):
        return jnp.sum(jnp.where(cols == rows, ck_new[h:h + 1, :], 0.0), axis=-1, keepdims=True)

    fm = dict(k_feature_major=True, v_feature_major=True)
    for p in range(N_PAIRS):
        sl = slice(p * LANES, (p + 1) * LANES)
        q_e, q_o = _split_pair(q1_ref[:, sl], even)
        k = pad_keys(kn1_ref[:, sl])
        v = pad_keys(vn1_ref[:, sl])
        r_e, a_e = _sb_step(q_e, k, v, u_new, zero_col, zero_acc, strict)
        r_o, a_o = _sb_step(q_o, k, v, u_new, zero_col, zero_acc, strict)
        for kb in reversed(range(n_cache_blocks)):
            ks = slice(kb * tk, (kb + 1) * tk)
            k = kc1_ref[0, sl, ks].astype(BF16)
            v = vc1_ref[0, sl, ks].astype(BF16)
            r_e, a_e = _sb_step(q_e, k, v, u, r_e, a_e, None, **fm)
            r_o, a_o = _sb_step(q_o, k, v, u, r_o, a_o, None, **fm)
        o = jnp.where(even, a_e, a_o)
        y_scr[:, sl] = _head_norm_pair(o, gsb_ref[:, sl], even).astype(BF16)

    neg_col = jnp.full((tq, 1), NEG_BIG, F32)
    for p in range(N_PAIRS):
        sl = slice(p * LANES, (p + 1) * LANES)
        q_e, q_o = _split_pair(q2_ref[:, sl], even)
        he, ho = 2 * p, 2 * p + 1
        cq_e, cq_o = query_col(he), query_col(ho)
        k = pad_keys(kn2_ref[:, sl])
        v = pad_keys(vn2_ref[:, sl])
        m_e, l_e, a_e = _fox_step(q_e, k, v, cq_e, ck_new[he:he + 1, :], neg_col, zero_col, zero_acc, causal)
        m_o, l_o, a_o = _fox_step(q_o, k, v, cq_o, ck_new[ho:ho + 1, :], neg_col, zero_col, zero_acc, causal)
        for kb in reversed(range(n_cache_blocks)):
            ks = slice(kb * tk, (kb + 1) * tk)
            k = kc2_ref[0, sl, ks].astype(BF16)
            v = vc2_ref[0, sl, ks].astype(BF16)
            m_e, l_e, a_e = _fox_step(q_e, k, v, cq_e, ck_cache[he:he + 1, ks], m_e, l_e, a_e, None, **fm)
            m_o, l_o, a_o = _fox_step(q_o, k, v, cq_o, ck_cache[ho:ho + 1, ks], m_o, l_o, a_o, None, **fm)
        o = jnp.where(even, a_e / l_e, a_o / l_o)
        y_scr[:, D_GRP + p * LANES:D_GRP + (p + 1) * LANES] = (
            _head_norm_pair(o, gfox_ref[:, sl], even).astype(BF16))

    h_ref[...] = x_ref[...] + _dot(y_scr[...], wout_ref[...])


def _attn_sample(q1, q2, kn1, vn1, kn2, vn2, kc1, vc1, kc2, vc2, clft, lfnt,
                 x2d, g_sb, g_fox, wout, batch, t_new, past):
    n = x2d.shape[0]
    tk = min(ATTN_TILE, past)
    assert past % tk == 0 and t_new <= NEW_KEY_PAD <= tk
    u = jnp.tri(tk, dtype=BF16)
    u_past = jnp.tri(past, dtype=BF16).T

    new = lambda w: pl.BlockSpec((t_new, w), lambda b: (b, 0))
    per_seq = lambda rows, cols: pl.BlockSpec((1, rows, cols), lambda b: (b, 0, 0))
    const = lambda a: pl.BlockSpec(a.shape, lambda b: (0,) * a.ndim)
    return pl.pallas_call(
        _attn_sample_kernel,
        grid=(batch,),
        in_specs=[new(D_GRP)] * 6 + [per_seq(D_GRP, past)] * 4
                 + [per_seq(N_HEADS, past), per_seq(N_HEADS, LANES),
                    new(D_MODEL), const(g_sb), const(g_fox), const(wout), const(u), const(u_past)],
        out_specs=new(D_MODEL),
        out_shape=jax.ShapeDtypeStruct((n, D_MODEL), F32),
        scratch_shapes=[pltpu.VMEM((t_new, D_MODEL), BF16)],
        compiler_params=_params("arbitrary"),
        name="attn_sample",
    )(q1, q2, kn1, vn1, kn2, vn2, kc1, vc1, kc2, vc2, clft, lfnt,
      x2d, g_sb, g_fox, wout, u, u_past)


def _store_token_tiles(ref, x):
    rows = x.shape[0]
    for j in range(ROW_TILES):
        ref[pl.ds(j, rows, stride=ROW_TILES), :] = x[:, j * LANES:(j + 1) * LANES]


def _load_token_tiles(ref, rows):
    return jnp.concatenate([ref[pl.ds(j, rows, stride=ROW_TILES), :] for j in range(ROW_TILES)], axis=-1)


def _router_kernel(h_ref, g_ref, wr_ref, br_ref, xn_ref, idx_ref, gate_ref):
    tm = h_ref.shape[0]
    xn = _rms(h_ref[...], g_ref[...])
    _store_token_tiles(xn_ref, xn)
    x_hi, x_lo = _split2(xn)
    w_hi, w_lo = _split2(wr_ref[...])
    logits = _dot(x_hi, w_hi) + _dot(x_lo, w_hi) + _dot(x_hi, w_lo) + br_ref[...]
    lane = lax.broadcasted_iota(I32, (tm, LANES), 1).astype(F32)
    vals, idxs = [], []
    for _ in range(TOP_K):
        m = jnp.max(logits, axis=-1, keepdims=True)
        am = jnp.min(jnp.where(logits == m, lane, float(LANES)), axis=-1, keepdims=True)
        vals.append(m)
        idxs.append(am)
        logits = jnp.where(lane == am, -jnp.inf, logits)
    exps = [jnp.exp(v - vals[0]) for v in vals]
    denom = exps[0] + exps[1] + exps[2] + exps[3]
    idx_out = jnp.zeros((tm, LANES), F32)
    gate_out = jnp.zeros((tm, LANES), F32)
    for k in range(TOP_K):
        idx_out = jnp.where(lane == float(k), idxs[k], idx_out)
        gate_out = jnp.where(lane == float(k), exps[k] / denom, gate_out)
    idx_ref[...] = idx_out[:, :TOP_K].astype(I32)
    gate_ref[...] = gate_out[:, :TOP_K]


def _router(h2d, g_ffn, wr_pad, br_pad):
    n = h2d.shape[0]
    tm = min(TOKEN_TILE, n)
    assert n % tm == 0
    const = lambda a: pl.BlockSpec(a.shape, lambda i: (0,) * a.ndim)
    return pl.pallas_call(
        _router_kernel,
        grid=(n // tm,),
        in_specs=[pl.BlockSpec((tm, D_MODEL), lambda i: (i, 0)), const(g_ffn), const(wr_pad), const(br_pad)],
        out_specs=[pl.BlockSpec((tm * ROW_TILES, LANES), lambda i: (i, 0)),
                   pl.BlockSpec((tm, TOP_K), lambda i: (i, 0)),
                   pl.BlockSpec((tm, TOP_K), lambda i: (i, 0))],
        out_shape=[jax.ShapeDtypeStruct((n * ROW_TILES, LANES), F32),
                   jax.ShapeDtypeStruct((n, TOP_K), I32),
                   jax.ShapeDtypeStruct((n, TOP_K), F32)],
        compiler_params=_params("arbitrary"),
        name="router",
    )(h2d, g_ffn, wr_pad, br_pad)


def _moe_kernel(block_e_ref, nvalid_ref, nact_ref, src0_ref, src1_ref, dst_ref,
                x_hbm, wup_ref, bup_ref, wdn_ref, bdn_ref, y_hbm,
                xbuf, ybuf, gsem, ssem, *, rows):
    i = pl.program_id(0)
    nact = nact_ref[0]
    slot = i % 2
    tile_rows = rows * ROW_TILES
    issue_unroll = 8

    def gather_start(src_ref, s):
        def group(g, carry):
            for j in range(issue_unroll):
                r = g * issue_unroll + j
                tok = pl.multiple_of(src_ref[0, 0, r] * ROW_TILES, ROW_TILES)
                pltpu.make_async_copy(x_hbm.at[pl.ds(tok, ROW_TILES), :],
                                      xbuf.at[s, pl.ds(r * ROW_TILES, ROW_TILES), :],
                                      gsem.at[s]).start(priority=j % 2)
            return carry
        lax.fori_loop(0, rows // issue_unroll, group, 0)

    def gather_wait(s):
        pltpu.make_async_copy(x_hbm.at[pl.ds(0, tile_rows), :], xbuf.at[s], gsem.at[s]).wait()

    def scatter_start(s, nv):
        def one(r, priority):
            dst = pl.multiple_of(dst_ref[0, 0, r] * ROW_TILES, ROW_TILES)
            pltpu.make_async_copy(ybuf.at[s, pl.ds(r * ROW_TILES, ROW_TILES), :],
                                  y_hbm.at[pl.ds(dst, ROW_TILES), :],
                                  ssem.at[s]).start(priority=priority)

        def group(g, carry):
            for j in range(issue_unroll):
                one(g * issue_unroll + j, j % 2)
            return carry

        def single(r, carry):
            one(r, 0)
            return carry

        full = nv // issue_unroll
        lax.fori_loop(0, full, group, 0)
        lax.fori_loop(full * issue_unroll, nv, single, 0)

    def scatter_wait(s, nv):
        n = nv * ROW_TILES
        pltpu.make_async_copy(ybuf.at[s, pl.ds(0, n), :], y_hbm.at[pl.ds(0, n), :], ssem.at[s]).wait()

    @pl.when(i < nact)
    def _():
        @pl.when(i == 0)
        def _():
            gather_start(src0_ref, 0)

        @pl.when(i + 1 < nact)
        def _():
            gather_start(src1_ref, 1 - slot)

        gather_wait(slot)
        x = _load_token_tiles(xbuf.at[slot], rows).astype(BF16)
        gu = _dot(x, wup_ref[0]) + bup_ref[0]
        gate = jnp.minimum(gu[:, :D_EXPERT], SWIGLU_LIMIT)
        up = jnp.clip(gu[:, D_EXPERT:], -SWIGLU_LIMIT, SWIGLU_LIMIT)
        hb = (up + 1.0) * gate * (1.0 / (1.0 + jnp.exp(-SWIGLU_ALPHA * gate)))
        y = _dot(hb.astype(BF16), wdn_ref[0]) + bdn_ref[0]

        @pl.when(i >= 2)
        def _():
            scatter_wait(slot, nvalid_ref[i - 2])

        _store_token_tiles(ybuf.at[slot], y)
        scatter_start(slot, nvalid_ref[i])

        @pl.when(i == nact - 1)
        def _():
            @pl.when(i >= 1)
            def _():
                scatter_wait(1 - slot, nvalid_ref[i - 1])
            scatter_wait(slot, nvalid_ref[i])


def _moe(xn_tiles, idx, wup, bup, wdn, bdn, n):
    nk = n * TOP_K
    mean_load = -(-nk // N_EXPERTS)
    rows = min(MOE_ROWS, max(MOE_MIN_ROWS, 1 << (mean_load - 1).bit_length()))
    assert n >= rows
    n_blocks = -(-nk // rows) + N_EXPERTS
    cap = n_blocks * rows

    flat_e = idx.reshape(-1)
    order = jnp.argsort(flat_e, stable=True).astype(I32)
    counts = jnp.sum(flat_e[:, None] == jnp.arange(N_EXPERTS, dtype=I32)[None, :], axis=0, dtype=I32)
    group_start = jnp.cumsum(counts) - counts
    padded = (counts + rows - 1) // rows * rows
    padded_end = jnp.cumsum(padded)
    padded_start = padded_end - padded
    nact = padded_end[-1] // rows
    blk = jnp.arange(n_blocks, dtype=I32)
    blk_first = jnp.minimum(blk, nact - 1) * rows
    blk_e = jnp.minimum(jnp.sum(padded_end[None, :] <= blk_first[:, None], axis=1, dtype=I32), N_EXPERTS - 1)
    slot = jnp.arange(cap, dtype=I32)
    slot_e = jnp.repeat(blk_e, rows)
    rank = slot - padded_start[slot_e]
    valid = (rank < counts[slot_e]) & (slot < padded_end[-1])
    flat = order[jnp.clip(group_start[slot_e] + rank, 0, nk - 1)]
    src = jnp.where(valid, flat // TOP_K, 0).astype(I32)
    dst = jnp.where(valid, (flat % TOP_K) * n + flat // TOP_K, 0).astype(I32)
    nvalid = jnp.sum(valid.reshape(n_blocks, rows), axis=1, dtype=I32)

    idx_block = lambda f: pl.BlockSpec((1, 1, rows), f, memory_space=pltpu.SMEM)
    grid_spec = pltpu.PrefetchScalarGridSpec(
        num_scalar_prefetch=3,
        grid=(n_blocks,),
        in_specs=[idx_block(lambda i, be, nv, na: (i, 0, 0)),
                  idx_block(lambda i, be, nv, na: (jnp.minimum(i + 1, n_blocks - 1), 0, 0)),
                  idx_block(lambda i, be, nv, na: (i, 0, 0)),
                  pl.BlockSpec(memory_space=pl.ANY),
                  pl.BlockSpec((1, D_MODEL, 2 * D_EXPERT), lambda i, be, nv, na: (be[i], 0, 0)),
                  pl.BlockSpec((1, 1, 2 * D_EXPERT), lambda i, be, nv, na: (be[i], 0, 0)),
                  pl.BlockSpec((1, D_EXPERT, D_MODEL), lambda i, be, nv, na: (be[i], 0, 0)),
                  pl.BlockSpec((1, 1, D_MODEL), lambda i, be, nv, na: (be[i], 0, 0))],
        out_specs=pl.BlockSpec(memory_space=pl.ANY),
        scratch_shapes=[pltpu.VMEM((2, rows * ROW_TILES, LANES), F32),
                        pltpu.VMEM((2, rows * ROW_TILES, LANES), F32),
                        pltpu.SemaphoreType.DMA((2,)),
                        pltpu.SemaphoreType.DMA((2,))],
    )
    src3 = src.reshape(n_blocks, 1, rows)
    dst3 = dst.reshape(n_blocks, 1, rows)
    return pl.pallas_call(
        functools.partial(_moe_kernel, rows=rows),
        grid_spec=grid_spec,
        out_shape=jax.ShapeDtypeStruct((nk * ROW_TILES, LANES), F32),
        compiler_params=_params("arbitrary"),
        name="moe",
    )(blk_e, nvalid, nact.reshape(1).astype(I32), src3, src3, dst3, xn_tiles, wup, bup, wdn, bdn)


def _final_kernel(h_ref, y0_ref, y1_ref, y2_ref, y3_ref, gate_ref, p_ref,
                  wg_ref, bg_ref, wp_ref, gf_ref, o_ref):
    tm = h_ref.shape[0]
    gates = gate_ref[...]
    h = h_ref[...]
    for k, y_ref in enumerate((y0_ref, y1_ref, y2_ref, y3_ref)):
        h = h + gates[:, k:k + 1] * _load_token_tiles(y_ref, tm)
    g = 1.0 / (1.0 + jnp.exp(-(_dot(h.astype(BF16), wg_ref[...]) + bg_ref[...])))
    h = h + g * _dot(p_ref[...].astype(BF16), wp_ref[...])
    o_ref[...] = _rms(h, gf_ref[...])


def _final(h2d, y_tiles, gates, p2d, wg, bg, wp, g_final):
    n = h2d.shape[0]
    tm = min(ATTN_TILE, n)
    assert n % tm == 0
    nt = n // tm
    const = lambda a: pl.BlockSpec(a.shape, lambda i: (0,) * a.ndim)
    y_spec = lambda k: pl.BlockSpec((tm * ROW_TILES, LANES), lambda i: (k * nt + i, 0))
    return pl.pallas_call(
        _final_kernel,
        grid=(nt,),
        in_specs=[pl.BlockSpec((tm, D_MODEL), lambda i: (i, 0)),
                  y_spec(0), y_spec(1), y_spec(2), y_spec(3),
                  pl.BlockSpec((tm, TOP_K), lambda i: (i, 0)),
                  pl.BlockSpec((tm, D_PLE), lambda i: (i, 0)),
                  const(wg), const(bg), const(wp), const(g_final)],
        out_specs=pl.BlockSpec((tm, D_MODEL), lambda i: (i, 0)),
        out_shape=jax.ShapeDtypeStruct((n, D_MODEL), F32),
        compiler_params=_params("arbitrary"),
        name="final",
    )(h2d, y_tiles, y_tiles, y_tiles, y_tiles, gates, p2d, wg, bg, wp, g_final)


def _pad_lanes(a):
    return jnp.pad(a, ((0, 0), (0, LANES - a.shape[-1])))


def _layer_weights(i, g_attn, w_in, b_f, g_sb, g_fox, w_out, g_ffn, w_router, b_router,
                   w_up, b_up, w_down, b_down, w_ple, w_ple_gate, b_ple_gate):
    n_qkv = 6 * D_GRP
    return dict(
        g_attn=g_attn[i][None, :],
        wqkv=w_in[i][:, :n_qkv].astype(BF16),
        wf=_pad_lanes(w_in[i][:, n_qkv:]),
        bf=_pad_lanes(b_f[i][None, :]),
        g_sb=g_sb[i][None, :], g_fox=g_fox[i][None, :],
        wout=w_out[i].astype(BF16),
        g_ffn=g_ffn[i][None, :],
        wr=_pad_lanes(w_router[i]),
        br=jnp.pad(b_router[i][None, :], ((0, 0), (0, LANES - N_EXPERTS)), constant_values=NEG_BIG),
        wup=w_up[i].astype(BF16), bup=b_up[i][:, None, :],
        wdn=w_down[i].astype(BF16), bdn=b_down[i][:, None, :],
        wg=w_ple_gate[i].astype(BF16), bg=b_ple_gate[i][None, :],
        wp=w_ple[i].astype(BF16),
    )


def _ffn_and_ple(h, p2d, w, g_final_or_none):
    n = h.shape[0]
    xn_tiles, idx, gates = _router(h, w["g_ffn"], w["wr"], w["br"])
    y_tiles = _moe(xn_tiles, idx, w["wup"], w["bup"], w["wdn"], w["bdn"], n)
    return _final(h, y_tiles, gates, p2d, w["wg"], w["bg"], w["wp"], g_final_or_none)


def kernel(x_prompt, x_sample, cache_sb_k, cache_sb_v, cache_fox_k, cache_fox_v, cache_fox_logf,
           p_prompt, p_sample, g_attn, w_in, b_f, g_sb, g_fox, w_out, g_ffn, w_router, b_router,
           w_up, b_up, w_down, b_down, w_ple, w_ple_gate, b_ple_gate, g_final):
    depth = w_in.shape[0]
    assert depth == 1, "the final norm is fused into the layer's last kernel"
    bp, tp, _ = x_prompt.shape
    bs, ts, _ = x_sample.shape
    past = cache_sb_k.shape[2]
    w = _layer_weights(0, g_attn, w_in, b_f, g_sb, g_fox, w_out, g_ffn, w_router, b_router,
                       w_up, b_up, w_down, b_down, w_ple, w_ple_gate, b_ple_gate)
    gfin = g_final[None, :]
    kv5 = lambda a, b, t: a.reshape(1, b, t, N_HEADS, HEAD_DIM)
    from_seq_major = lambda a, b, t: jnp.transpose(a.reshape(b, N_HEADS, HEAD_DIM, t), (0, 3, 1, 2))[None]
    to_seq_major = lambda a, b, t: jnp.transpose(a[0], (0, 2, 3, 1)).reshape(b, D_GRP, t)

    xp = x_prompt.reshape(bp * tp, D_MODEL)
    (q1, k1t, v1t, q2, k2t, v2t, lft, kb1t, vb1, kb2t, vb2, c, ct) = _inproj_seq(
        xp, w["g_attn"], w["wqkv"], w["wf"], w["bf"], bp, tp)
    hp = _attn_prompt(q1, q2, kb1t, vb1, kb2t, vb2, c, ct, xp, w["g_sb"], w["g_fox"], w["wout"], bp, tp)
    yp = _ffn_and_ple(hp, p_prompt[0].reshape(bp * tp, D_PLE), w, gfin)
    out_p = (from_seq_major(k1t, bp, tp), from_seq_major(v1t, bp, tp),
             from_seq_major(k2t, bp, tp), from_seq_major(v2t, bp, tp),
             jnp.swapaxes(lft, 1, 2)[None])

    xs = x_sample.reshape(bs * ts, D_MODEL)
    (q1, k1, v1, q2, k2, v2, lf, kb1, vb1, kb2, vb2, lft) = _inproj_rows(
        xs, w["g_attn"], w["wqkv"], w["wf"], w["bf"])
    clft = jnp.swapaxes(cache_fox_logf[0].astype(F32), 1, 2)
    lfnt = jnp.swapaxes(lft, 0, 1).reshape(N_HEADS, bs, ts)
    lfnt = jnp.pad(jnp.swapaxes(lfnt, 0, 1), ((0, 0), (0, 0), (0, LANES - ts)))
    hs = _attn_sample(q1, q2, kb1, vb1, kb2, vb2,
                      to_seq_major(cache_sb_k, bs, past), to_seq_major(cache_sb_v, bs, past),
                      to_seq_major(cache_fox_k, bs, past), to_seq_major(cache_fox_v, bs, past),
                      clft, lfnt, xs, w["g_sb"], w["g_fox"], w["wout"], bs, ts, past)
    ys = _ffn_and_ple(hs, p_sample[0].reshape(bs * ts, D_PLE), w, gfin)
    out_s = (kv5(k1, bs, ts), kv5(v1, bs, ts), kv5(k2, bs, ts), kv5(v2, bs, ts),
             lf.reshape(1, bs, ts, N_HEADS))

    return (yp.reshape(bp, tp, D_MODEL), ys.reshape(bs, ts, D_MODEL)) + out_p + out_s
```
